```python
import math
import functools
import jax
import jax.numpy as jnp
from jax import lax
import numpy as np


D_MODEL = 1024
BATCH = 2
SEQ = 8192
DEPTH = 2

MLSTM_HEADS = 4
MLSTM_QK_DIM = 64
MLSTM_V_DIM = 128
MLSTM_CHUNK = 128
ATTN_HEADS = 8
ATTN_HEAD_DIM = 64
DILATED_BRANCHES = ((128, 1), (512, 4), (2048, 16))
ATTN_BLOCK = 128
REL_BUCKETS = 32
REL_MAX_DIST = 2048
CONV_CH = 512
CONV_K = 3
SGU_CH = 512
SGU_GROUPS = 4
SGU_GROUP_CH = SGU_CH // SGU_GROUPS
SGU_CHUNK = 128
D_FF = 2816
FFN_RES_W = 0.5
LN_EPS = 1e-5
DEEPNORM_ALPHA = (2 * DEPTH) ** 0.25
DEEPNORM_BETA = (8 * DEPTH) ** -0.25
N_EVEN = (DEPTH + 1) // 2
N_ODD = DEPTH // 2
MLSTM_QK_W = MLSTM_HEADS * MLSTM_QK_DIM
MLSTM_V_W = MLSTM_HEADS * MLSTM_V_DIM
ATTN_W = ATTN_HEADS * ATTN_HEAD_DIM
AB_SPLITS = (MLSTM_QK_W, MLSTM_QK_W, MLSTM_V_W, MLSTM_V_W, MLSTM_HEADS, MLSTM_HEADS, ATTN_W, ATTN_W, ATTN_W)
AB_IN_W = sum(AB_SPLITS)
AB_MIX_W = MLSTM_V_W + ATTN_W
CD_SPLITS = (CONV_CH, CONV_CH, CONV_CH, SGU_CH, SGU_CH)
CD_IN_W = sum(CD_SPLITS)
CD_MIX_W = CONV_CH + SGU_CH

kernel_name = 'hybrid_mlstm_dilated_shortconv_sgu_macaron'


def _split_cols(z, sizes):
    idx = np.cumsum(sizes)[:-1].tolist()
    return jnp.split(z, idx, axis=-1)


def layer_norm(x, g, b):
    xf = x.astype(jnp.float32)
    mu = jnp.mean(xf, axis=-1, keepdims=True)
    var = jnp.mean(jnp.square(xf - mu), axis=-1, keepdims=True)
    return ((xf - mu) * lax.rsqrt(var + LN_EPS) * g + b).astype(x.dtype)


def swiglu(h, w_gate, w_up, w_down):
    return (jax.nn.silu(h @ w_gate) * (h @ w_up)) @ w_down


def rel_bucket(dist):
    max_exact = REL_BUCKETS // 2
    d = jnp.maximum(dist, 0)
    large = max_exact + (jnp.log(jnp.maximum(d, 1).astype(jnp.float32) / max_exact)
                         / math.log(REL_MAX_DIST / max_exact) * (REL_BUCKETS - max_exact)).astype(jnp.int32)
    large = jnp.minimum(large, REL_BUCKETS - 1)
    return jnp.where(d < max_exact, d, large)


def mlstm_chunkwise(q, k, v, i_pre, f_pre):
    B, S, H, DK = q.shape
    DV = v.shape[-1]
    L = MLSTM_CHUNK
    NC = S // L
    f32 = jnp.float32
    q = q.astype(f32).reshape(B, NC, L, H, DK)
    k = (k.astype(f32) * DK ** -0.5).reshape(B, NC, L, H, DK)
    v = v.astype(f32).reshape(B, NC, L, H, DV)
    log_i = i_pre.astype(f32).reshape(B, NC, L, H).transpose(0, 1, 3, 2)
    log_f = jax.nn.log_sigmoid(f_pre.astype(f32)).reshape(B, NC, L, H).transpose(0, 1, 3, 2)
    cum_f = jnp.cumsum(log_f, axis=-1)
    causal = jnp.tril(jnp.ones((L, L), dtype=bool))
    d_log = jnp.where(causal, cum_f[..., :, None] - cum_f[..., None, :] + log_i[..., None, :], -jnp.inf)
    chunk_f = cum_f[..., -1]
    g = chunk_f[..., None] - cum_f + log_i
    g_max = jnp.max(g, axis=-1)

    def step(carry, xs):
        C, n, m = carry
        g_c, gmax_c, f_c, k_c, v_c = xs
        m_new = jnp.maximum(f_c + m, gmax_c)
        decay = jnp.exp(f_c + m - m_new)
        w = jnp.exp(g_c - m_new[..., None])
        C_new = decay[..., None, None] * C + jnp.einsum('bhl,blhk,blhv->bhkv', w, k_c, v_c)
        n_new = decay[..., None] * n + jnp.einsum('bhl,blhk->bhk', w, k_c)
        return (C_new, n_new, m_new), (C, n, m)

    init = (jnp.zeros((B, H, DK, DV), f32), jnp.zeros((B, H, DK), f32), jnp.zeros((B, H), f32))
    xs = (jnp.moveaxis(g, 1, 0), jnp.moveaxis(g_max, 1, 0), jnp.moveaxis(chunk_f, 1, 0),
          jnp.moveaxis(k, 1, 0), jnp.moveaxis(v, 1, 0))
    _, (C_prev, n_prev, m_prev) = lax.scan(step, init, xs)
    C_prev = jnp.moveaxis(C_prev, 0, 1)
    n_prev = jnp.moveaxis(n_prev, 0, 1)
    m_prev = jnp.moveaxis(m_prev, 0, 1)
    inter_log = cum_f + m_prev[..., None]
    m_t = jnp.maximum(inter_log, jnp.max(d_log, axis=-1))
    w_intra = jnp.exp(d_log - m_t[..., None]) * jnp.einsum('bcthk,bcshk->bchts', q, k)
    w_inter = jnp.exp(inter_log - m_t)
    num = (jnp.einsum('bchts,bcshv->bcthv', w_intra, v)
           + jnp.einsum('bcthk,bchkv->bcthv', q, C_prev) * w_inter.transpose(0, 1, 3, 2)[..., None])
    den = jnp.sum(w_intra, axis=-1) + jnp.einsum('bcthk,bchk->bcht', q, n_prev) * w_inter
    den = jnp.maximum(jnp.abs(den), jnp.exp(-m_t))
    h = num / den.transpose(0, 1, 3, 2)[..., None]
    return h.reshape(B, S, H, DV)


def dilated_branch(q, k, v, rel_bias, window, dilation):
    B, S, H, Dh = q.shape
    L = S // dilation
    span = window // dilation
    blk = ATTN_BLOCK
    nb = -(-L // blk)
    Lp = nb * blk

    def to_sub(t):
        t = t.reshape(B, L, dilation, H, Dh).transpose(0, 2, 1, 3, 4)
        return jnp.pad(t, ((0, 0), (0, 0), (0, Lp - L), (0, 0), (0, 0)))

    def windows(t):
        t = jnp.pad(t, ((0, 0), (0, 0), (blk, 0), (0, 0), (0, 0))).reshape(B, dilation, nb + 1, blk, H, Dh)
        return jnp.concatenate([t[:, :, :-1], t[:, :, 1:]], axis=3)

    qb = to_sub(q).reshape(B, dilation, nb, blk, H, Dh)
    kw = windows(to_sub(k))
    vw = windows(to_sub(v))
    qi = jnp.arange(blk)[:, None]
    kj = jnp.arange(2 * blk)[None, :]
    dist = qi + blk - kj
    key_idx = jnp.arange(nb)[:, None, None] * blk - blk + kj[None]
    valid = (dist >= 0) & (dist <= span) & (key_idx >= 0)
    bias = rel_bias[rel_bucket(dist * dilation)].transpose(2, 0, 1).astype(jnp.float32)
    s = jnp.einsum('brnqhd,brnkhd->brnhqk', qb, kw, preferred_element_type=jnp.float32) * (Dh ** -0.5)
    s = jnp.where(valid[None, None, :, None], s + bias, -1e30)
    m = jnp.max(s, axis=-1, keepdims=True)
    p = jnp.exp(s - m)
    l = jnp.sum(p, axis=-1, keepdims=True)
    o = jnp.einsum('brnhqk,brnkhd->brnhqd', p, vw.astype(jnp.float32)) / l
    lse = (m + jnp.log(l))[..., 0]
    o = o.transpose(0, 1, 2, 4, 3, 5).reshape(B, dilation, Lp, H, Dh)[:, :, :L]
    o = o.transpose(0, 2, 1, 3, 4).reshape(B, S, H, Dh)
    lse = lse.transpose(0, 1, 2, 4, 3).reshape(B, dilation, Lp, H)[:, :, :L]
    lse = lse.transpose(0, 2, 1, 3).reshape(B, S, H)
    return o, lse


def dilated_attention(q, k, v, rel_bias):
    outs, lses = [], []
    for window, dilation in DILATED_BRANCHES:
        o, lse = dilated_branch(q, k, v, rel_bias, window, dilation)
        outs.append(o)
        lses.append(lse)
    w = jax.nn.softmax(jnp.stack(lses, axis=0), axis=0)
    return jnp.einsum('gbsh,gbshd->bshd', w, jnp.stack(outs, axis=0))


def mixer_mlstm_dilated(h, w_in, w_out, b_igate, b_fgate, rel_bias):
    B, S, _ = h.shape
    z = h @ w_in
    mq, mk, mv, mo, mi, mf, aq, ak, av = _split_cols(z, AB_SPLITS)
    h_a = mlstm_chunkwise(mq.reshape(B, S, MLSTM_HEADS, MLSTM_QK_DIM), mk.reshape(B, S, MLSTM_HEADS, MLSTM_QK_DIM),
                          mv.reshape(B, S, MLSTM_HEADS, MLSTM_V_DIM), mi + b_igate, mf + b_fgate)
    h_a = jax.nn.sigmoid(mo.astype(jnp.float32)) * h_a.reshape(B, S, MLSTM_V_W)
    h_b = dilated_attention(aq.reshape(B, S, ATTN_HEADS, ATTN_HEAD_DIM), ak.reshape(B, S, ATTN_HEADS, ATTN_HEAD_DIM),
                            av.reshape(B, S, ATTN_HEADS, ATTN_HEAD_DIM), rel_bias).reshape(B, S, ATTN_W)
    return jnp.concatenate([h_a, h_b], axis=-1).astype(h.dtype) @ w_out


def causal_depthwise_conv(u, w, b):
    K, C = w.shape
    y = lax.conv_general_dilated(u, w[:, None, :].astype(u.dtype), window_strides=(1,), padding=[(K - 1, 0)],
                                 dimension_numbers=('NWC', 'WIO', 'NWC'), feature_group_count=C)
    return y + b


def mixer_shortconv_sgu(h, w_in, w_out, conv_w, conv_b, sgu_ln_g, sgu_ln_b, sgu_w, sgu_b):
    B, S, _ = h.shape
    z = h @ w_in
    gate_b, gate_c, xc, u, v = _split_cols(z, CD_SPLITS)
    y_c = gate_b * causal_depthwise_conv(gate_c * xc, conv_w, conv_b)
    u = jax.nn.gelu(u)
    v = jax.nn.gelu(v).reshape(B, S, SGU_GROUPS, SGU_GROUP_CH)
    v = layer_norm(v, sgu_ln_g, sgu_ln_b)
    v = v.reshape(B, S // SGU_CHUNK, SGU_CHUNK, SGU_GROUPS, SGU_GROUP_CH)
    w_s = jnp.where(jnp.tril(jnp.ones((SGU_CHUNK, SGU_CHUNK), dtype=bool)), sgu_w, 0.0)
    mixed = jnp.einsum('gts,bnsgc->bntgc', w_s.astype(v.dtype), v) + sgu_b.T[:, :, None]
    y_d = u * mixed.reshape(B, S, SGU_CH)
    return jnp.concatenate([y_c, y_d], axis=-1) @ w_out


def _ada_post_norm(x, mod, fn, ln_g, ln_b, res_w):
    shift, scale, gate = mod[:, 0, None], mod[:, 1, None], mod[:, 2, None]
    y = fn(x * (1 + scale) + shift)
    return layer_norm(DEEPNORM_ALPHA * x + res_w * (1 + gate) * y, ln_g, ln_b)


def setup_inputs(seed: int = 0) -> dict:
    key = jax.random.key(seed)
    ks = jax.random.split(key, 24)
    f32 = jnp.float32
    D = D_MODEL

    def nrm(k, shape, s):
        return jax.random.normal(k, shape, f32) * s

    return {
        'x': nrm(ks[0], (BATCH, SEQ, D), 1.0),
        'c': nrm(ks[1], (BATCH, D), 1.0),
        'rel_bias': nrm(ks[2], (REL_BUCKETS, ATTN_HEADS), 0.5),
        'ada_w': nrm(ks[3], (DEPTH, D, 9 * D), 0.2 * D ** -0.5),
        'ada_b': nrm(ks[4], (DEPTH, 9 * D), 0.02),
        'ln_g': 1.0 + nrm(ks[5], (DEPTH, 3, D), 0.02),
        'ln_b': nrm(ks[6], (DEPTH, 3, D), 0.02),
        'ffn_w_gate': nrm(ks[7], (DEPTH, 2, D, D_FF), D ** -0.5),
        'ffn_w_up': nrm(ks[8], (DEPTH, 2, D, D_FF), D ** -0.5),
        'ffn_w_down': nrm(ks[9], (DEPTH, 2, D_FF, D), DEEPNORM_BETA * D_FF ** -0.5),
        'ab_w_in': nrm(ks[10], (N_EVEN, D, AB_IN_W), D ** -0.5),
        'ab_w_out': nrm(ks[11], (N_EVEN, AB_MIX_W, D), DEEPNORM_BETA * AB_MIX_W ** -0.5),
        'ab_b_igate': nrm(ks[12], (N_EVEN, MLSTM_HEADS), 0.1),
        'ab_b_fgate': jnp.linspace(3.0, 6.0, MLSTM_HEADS, dtype=f32)[None] + nrm(ks[13], (N_EVEN, MLSTM_HEADS), 0.1),
        'cd_w_in': nrm(ks[14], (N_ODD, D, CD_IN_W), D ** -0.5),
        'cd_w_out': nrm(ks[15], (N_ODD, CD_MIX_W, D), DEEPNORM_BETA * CD_MIX_W ** -0.5),
        'cd_conv_w': nrm(ks[16], (N_ODD, CONV_K, CONV_CH), CONV_K ** -0.5),
        'cd_conv_b': nrm(ks[17], (N_ODD, CONV_CH), 0.02),
        'cd_sgu_ln_g': 1.0 + nrm(ks[18], (N_ODD, SGU_GROUPS, SGU_GROUP_CH), 0.02),
        'cd_sgu_ln_b': nrm(ks[19], (N_ODD, SGU_GROUPS, SGU_GROUP_CH), 0.02),
        'cd_sgu_w': nrm(ks[20], (N_ODD, SGU_GROUPS, SGU_CHUNK, SGU_CHUNK), SGU_CHUNK ** -0.5),
        'cd_sgu_b': 1.0 + nrm(ks[21], (N_ODD, SGU_GROUPS, SGU_CHUNK), 0.1),
    }


def reference(x, c, rel_bias, ada_w, ada_b, ln_g, ln_b, ffn_w_gate, ffn_w_up, ffn_w_down,
              ab_w_in, ab_w_out, ab_b_igate, ab_b_fgate,
              cd_w_in, cd_w_out, cd_conv_w, cd_conv_b, cd_sgu_ln_g, cd_sgu_ln_b, cd_sgu_w, cd_sgu_b):
    B = x.shape[0]
    for layer in range(DEPTH):
        mod = (jax.nn.silu(c) @ ada_w[layer] + ada_b[layer]).reshape(B, 3, 3, D_MODEL)
        ffn_pre = functools.partial(swiglu, w_gate=ffn_w_gate[layer, 0], w_up=ffn_w_up[layer, 0], w_down=ffn_w_down[layer, 0])
        ffn_post = functools.partial(swiglu, w_gate=ffn_w_gate[layer, 1], w_up=ffn_w_up[layer, 1], w_down=ffn_w_down[layer, 1])
        if layer % 2 == 0:
            e = layer // 2
            mix = functools.partial(mixer_mlstm_dilated, w_in=ab_w_in[e], w_out=ab_w_out[e],
                                    b_igate=ab_b_igate[e], b_fgate=ab_b_fgate[e], rel_bias=rel_bias)
        else:
            o = layer // 2
            mix = functools.partial(mixer_shortconv_sgu, w_in=cd_w_in[o], w_out=cd_w_out[o], conv_w=cd_conv_w[o],
                                    conv_b=cd_conv_b[o], sgu_ln_g=cd_sgu_ln_g[o], sgu_ln_b=cd_sgu_ln_b[o],
                                    sgu_w=cd_sgu_w[o], sgu_b=cd_sgu_b[o])
        x = _ada_post_norm(x, mod[:, 0], ffn_pre, ln_g[layer, 0], ln_b[layer, 0], FFN_RES_W)
        x = _ada_post_norm(x, mod[:, 1], mix, ln_g[layer, 1], ln_b[layer, 1], 1.0)
        x = _ada_post_norm(x, mod[:, 2], ffn_post, ln_g[layer, 2], ln_b[layer, 2], FFN_RES_W)
    return x
```

```python
import functools
import math

import jax
import jax.numpy as jnp
import numpy as np
from jax import lax
from jax.experimental import pallas as pl
from jax.experimental.pallas import tpu as pltpu

F32 = jnp.float32
BF16 = jnp.bfloat16

MLSTM_HEADS = 4
MLSTM_QK_DIM = 64
MLSTM_V_DIM = 128
MLSTM_CHUNK = 128
ATTN_HEADS = 8
ATTN_HEAD_DIM = 64
DILATED_BRANCHES = ((128, 1), (512, 4), (2048, 16))
ATTN_BLOCK = 128
REL_BUCKETS = 32
REL_MAX_DIST = 2048
CONV_CH = 512
CONV_K = 3
SGU_CH = 512
SGU_GROUPS = 4
SGU_GROUP_CH = SGU_CH // SGU_GROUPS
SGU_CHUNK = 128
FFN_RES_W = 0.5
LN_EPS = 1e-5
MASK_VALUE = -1e30

MLSTM_QK_W = MLSTM_HEADS * MLSTM_QK_DIM
MLSTM_V_W = MLSTM_HEADS * MLSTM_V_DIM
ATTN_W = ATTN_HEADS * ATTN_HEAD_DIM
GATE_W = 128

LANES = 128
SUBLANES = 8
VMEM_LIMIT_BYTES = 56 * 1024 * 1024

ROW_TILE = 512
FF_CHUNK = 256


def _const_spec(shape):
    nd = len(shape)
    return pl.BlockSpec(shape, lambda *_: (0,) * nd, pipeline_mode=pl.Buffered(1))


def _params(sem):
    return pltpu.CompilerParams(dimension_semantics=sem, vmem_limit_bytes=VMEM_LIMIT_BYTES)


def _modulate(x, mod_ref, sub):
    shift = mod_ref[3 * sub:3 * sub + 1, :]
    scale = mod_ref[3 * sub + 1:3 * sub + 2, :]
    return x * (1.0 + scale) + shift


def _post_norm(x, y, mod_ref, sub, g_ref, b_ref, alpha, res_w):
    gate = mod_ref[3 * sub + 2:3 * sub + 3, :]
    r = alpha * x + res_w * (1.0 + gate) * y
    mu = jnp.mean(r, axis=-1, keepdims=True)
    rc = r - mu
    var = jnp.mean(rc * rc, axis=-1, keepdims=True)
    return rc * lax.rsqrt(var + LN_EPS) * g_ref[...] + b_ref[...]


def _mod_kernel(c_ref, w_ref, b_ref, o_ref):
    c = c_ref[...]
    sc = c * jax.nn.sigmoid(c)
    o_ref[...] = jnp.dot(sc, w_ref[...], preferred_element_type=F32) + b_ref[...]


def _modulation(c, ada_w, ada_b):
    depth, d, n = ada_w.shape
    b = c.shape[0]
    bp = -(-b // SUBLANES) * SUBLANES
    cp = jnp.pad(c, ((0, bp - b), (0, 0)))
    bn = 1024
    out = pl.pallas_call(
        _mod_kernel,
        grid=(depth, n // bn),
        in_specs=[
            pl.BlockSpec((bp, d), lambda l, j: (0, 0)),
            pl.BlockSpec((None, d, bn), lambda l, j: (l, 0, j)),
            pl.BlockSpec((None, 1, bn), lambda l, j: (l, 0, j)),
        ],
        out_specs=pl.BlockSpec((None, bp, bn), lambda l, j: (l, 0, j)),
        out_shape=jax.ShapeDtypeStruct((depth, bp, n), F32),
        compiler_params=_params(("arbitrary", "arbitrary")),
        name="adaln_mod",
    )(cp, ada_w, ada_b.reshape(depth, 1, n))
    return out[:, :b].reshape(depth, b, 9, d)


def _ffn_kernel(x_ref, mod_ref, g_ref, b_ref, wg_ref, wu_ref, wd_ref, o_ref, *, sub, alpha):
    x = x_ref[...]
    h = _modulate(x, mod_ref, sub).astype(BF16)
    d_ff = wg_ref.shape[1]
    acc = jnp.zeros(x.shape, F32)
    for c0 in range(0, d_ff, FF_CHUNK):
        g = jnp.dot(h, wg_ref[:, c0:c0 + FF_CHUNK], preferred_element_type=F32)
        u = jnp.dot(h, wu_ref[:, c0:c0 + FF_CHUNK], preferred_element_type=F32)
        a = (g * jax.nn.sigmoid(g) * u).astype(BF16)
        acc = acc + jnp.dot(a, wd_ref[c0:c0 + FF_CHUNK, :], preferred_element_type=F32)
    o_ref[...] = _post_norm(x, acc, mod_ref, sub, g_ref, b_ref, alpha, FFN_RES_W)


def _ffn_sublayer(x2, mod, ln_g, ln_b, wg, wu, wd, *, sub, alpha, seq):
    rows, d = x2.shape
    d_ff = wg.shape[1]
    tiles_per_seq = seq // ROW_TILE
    return pl.pallas_call(
        functools.partial(_ffn_kernel, sub=sub, alpha=alpha),
        grid=(rows // ROW_TILE,),
        in_specs=[
            pl.BlockSpec((ROW_TILE, d), lambda i: (i, 0)),
            pl.BlockSpec((None, 9, d), lambda i: (i // tiles_per_seq, 0, 0)),
            _const_spec((1, d)),
            _const_spec((1, d)),
            _const_spec((d, d_ff)),
            _const_spec((d, d_ff)),
            _const_spec((d_ff, d)),
        ],
        out_specs=pl.BlockSpec((ROW_TILE, d), lambda i: (i, 0)),
        out_shape=jax.ShapeDtypeStruct((rows, d), F32),
        compiler_params=_params(("arbitrary",)),
        name="swiglu_sublayer",
    )(x2, mod, ln_g, ln_b, wg, wu, wd)


_AB_SEGS = (
    ("mq", MLSTM_QK_W, BF16, 1.0),
    ("mk", MLSTM_QK_W, BF16, MLSTM_QK_DIM ** -0.5),
    ("mv", MLSTM_V_W, BF16, 1.0),
    ("mo", MLSTM_V_W, F32, 1.0),
    ("gates", GATE_W, F32, 1.0),
    ("aq", ATTN_W, BF16, ATTN_HEAD_DIM ** -0.5),
    ("ak", ATTN_W, BF16, 1.0),
    ("av", ATTN_W, BF16, 1.0),
)
_AB_PACKED_W = sum(s[1] for s in _AB_SEGS)


def _pack_ab_w_in(w_in):
    d = w_in.shape[0]
    sizes = (MLSTM_QK_W, MLSTM_QK_W, MLSTM_V_W, MLSTM_V_W, MLSTM_HEADS, MLSTM_HEADS, ATTN_W, ATTN_W, ATTN_W)
    mq, mk, mv, mo, mi, mf, aq, ak, av = jnp.split(w_in, np.cumsum(sizes)[:-1].tolist(), axis=1)
    gates = jnp.concatenate([mi, mf, jnp.zeros((d, GATE_W - 2 * MLSTM_HEADS), w_in.dtype)], axis=1)
    return jnp.concatenate([mq, mk, mv, mo, gates, aq, ak, av], axis=1).astype(BF16)


def _ab_proj_kernel(x_ref, mod_ref, w_ref, *out_refs, sub):
    h = _modulate(x_ref[...], mod_ref, sub).astype(BF16)
    off = 0
    for (_, width, dtype, scale), o_ref in zip(_AB_SEGS, out_refs):
        z = jnp.dot(h, w_ref[:, off:off + width], preferred_element_type=F32)
        if scale != 1.0:
            z = z * scale
        o_ref[...] = z.astype(dtype)
        off += width


def _ab_projection(x2, mod, w_packed, *, sub, seq):
    rows, d = x2.shape
    tiles_per_seq = seq // ROW_TILE
    return pl.pallas_call(
        functools.partial(_ab_proj_kernel, sub=sub),
        grid=(rows // ROW_TILE,),
        in_specs=[
            pl.BlockSpec((ROW_TILE, d), lambda i: (i, 0)),
            pl.BlockSpec((None, 9, d), lambda i: (i // tiles_per_seq, 0, 0)),
            _const_spec((d, _AB_PACKED_W)),
        ],
        out_specs=[pl.BlockSpec((ROW_TILE, s[1]), lambda i: (i, 0)) for s in _AB_SEGS],
        out_shape=[jax.ShapeDtypeStruct((rows, s[1]), s[2]) for s in _AB_SEGS],
        compiler_params=_params(("arbitrary",)),
        name="mixer_ab_in_proj",
    )(x2, mod, w_packed)


def _mlstm_kernel(q_ref, k_ref, v_ref, og_ref, gates_ref, gbias_ref, out_ref, c_ref, n_ref, m_ref):
    L = MLSTM_CHUNK
    H = MLSTM_HEADS
    DK = MLSTM_QK_DIM
    DV = MLSTM_V_DIM

    @pl.when(pl.program_id(1) == 0)
    def _():
        c_ref[...] = jnp.zeros_like(c_ref)
        n_ref[...] = jnp.zeros_like(n_ref)
        m_ref[...] = jnp.zeros_like(m_ref)

    row = lax.broadcasted_iota(jnp.int32, (L, L), 0)
    col = lax.broadcasted_iota(jnp.int32, (L, L), 1)
    causal = row >= col

    gb = gates_ref[...] + gbias_ref[...]
    log_f = -(jnp.maximum(-gb, 0.0) + jnp.log1p(jnp.exp(-jnp.abs(gb))))
    log_f = jnp.where((col >= H) & (col < 2 * H), log_f, 0.0)
    cum_f = jnp.dot(causal.astype(F32), log_f, precision=lax.Precision.HIGHEST,
                    preferred_element_type=F32)
    a = jnp.where(col < H, gb, cum_f)
    a_t = a.T

    w_cols = []
    decays = []
    m_prevs = []
    for h in range(H):
        li_col = a[:, h:h + 1]
        cf_col = a[:, H + h:H + h + 1]
        chunk_f = cf_col[L - 1:L, :]
        m_prev = m_ref[h][0:1, 0:1]
        g_col = chunk_f - cf_col + li_col
        g_max = jnp.max(g_col, axis=0, keepdims=True)
        m_new = jnp.maximum(chunk_f + m_prev, g_max)
        decays.append(jnp.exp(chunk_f + m_prev - m_new))
        w_cols.append(jnp.broadcast_to(jnp.exp(g_col - m_new), (L, DK)))
        m_prevs.append(m_prev)
        m_ref[h] = jnp.broadcast_to(m_new, m_ref.shape[1:])
    kw = k_ref[...].astype(F32) * jnp.concatenate(w_cols, axis=1)
    kw_t = kw.T

    for h in range(H):
        q = q_ref[:, h * DK:(h + 1) * DK]
        k = k_ref[:, h * DK:(h + 1) * DK]
        v = v_ref[:, h * DV:(h + 1) * DV]
        li_row = a_t[h:h + 1, :]
        cf_row = a_t[H + h:H + h + 1, :]
        cf_col = a[:, H + h:H + h + 1]
        m_prev = m_prevs[h]
        c_prev = c_ref[h]
        n_prev = n_ref[h]

        d_log = jnp.where(causal, cf_col - cf_row + li_row, -jnp.inf)
        inter_log = cf_col + m_prev
        m_t = jnp.maximum(inter_log, jnp.max(d_log, axis=1, keepdims=True))
        s = lax.dot_general(q, k, (((1,), (1,)), ((), ())), preferred_element_type=F32)
        w_intra = jnp.exp(d_log - m_t) * s
        w_inter = jnp.exp(inter_log - m_t)
        num = (jnp.dot(w_intra.astype(BF16), v, preferred_element_type=F32)
               + jnp.dot(q, c_prev.astype(BF16), preferred_element_type=F32) * w_inter)
        den = (jnp.sum(w_intra, axis=1, keepdims=True)
               + jnp.sum(q.astype(F32) * n_prev, axis=1, keepdims=True) * w_inter)
        den = jnp.maximum(jnp.abs(den), jnp.exp(-m_t))
        o_gate = jax.nn.sigmoid(og_ref[:, h * DV:(h + 1) * DV])
        out_ref[:, h * DV:(h + 1) * DV] = o_gate * (num / den)

        kw_t_h = kw_t[h * DK:(h + 1) * DK, :]
        c_ref[h] = decays[h] * c_prev + jnp.dot(kw_t_h.astype(BF16), v, preferred_element_type=F32)
        n_ref[h] = decays[h] * n_prev + jnp.sum(kw[:, h * DK:(h + 1) * DK], axis=0, keepdims=True)


def _mlstm(mq, mk, mv, mo, gates, gate_bias, *, batch, seq):
    L = MLSTM_CHUNK
    nc = seq // L

    def spec(w):
        return pl.BlockSpec((L, w), lambda b, c: (b * nc + c, 0))

    return pl.pallas_call(
        _mlstm_kernel,
        grid=(batch, nc),
        in_specs=[spec(MLSTM_QK_W), spec(MLSTM_QK_W), spec(MLSTM_V_W), spec(MLSTM_V_W), spec(GATE_W),
                  pl.BlockSpec((1, GATE_W), lambda b, c: (0, 0))],
        out_specs=spec(MLSTM_V_W),
        out_shape=jax.ShapeDtypeStruct((batch * seq, MLSTM_V_W), F32),
        scratch_shapes=[
            pltpu.VMEM((MLSTM_HEADS, MLSTM_QK_DIM, MLSTM_V_DIM), F32),
            pltpu.VMEM((MLSTM_HEADS, 1, MLSTM_QK_DIM), F32),
            pltpu.VMEM((MLSTM_HEADS, SUBLANES, LANES), F32),
        ],
        compiler_params=_params(("arbitrary", "arbitrary")),
        name="mlstm_chunkwise",
    )(mq, mk, mv, mo, gates, gate_bias)


def _rel_bucket(dist):
    max_exact = REL_BUCKETS // 2
    d = jnp.maximum(dist, 0)
    large = max_exact + (jnp.log(jnp.maximum(d, 1).astype(F32) / max_exact)
                         / math.log(REL_MAX_DIST / max_exact) * (REL_BUCKETS - max_exact)).astype(jnp.int32)
    large = jnp.minimum(large, REL_BUCKETS - 1)
    return jnp.where(d < max_exact, d, large)


def _branch_bias(rel_bias, window, dilation):
    blk = ATTN_BLOCK
    span = window // dilation
    qi = jnp.arange(blk)[:, None]
    kj = jnp.arange(2 * blk)[None, :]
    dist = qi + blk - kj
    valid = (dist >= 0) & (dist <= span)
    bias = rel_bias[_rel_bucket(dist * dilation)].transpose(2, 0, 1).astype(F32)
    return jnp.where(valid[None], bias, MASK_VALUE)


def _dilated_kernel(q_ref, kp_ref, kc_ref, vp_ref, vc_ref, bias_ref, o_ref, lse_ref):
    blk = ATTN_BLOCK
    dh = ATTN_HEAD_DIM
    has_prev = pl.program_id(2) > 0
    nt = (((1,), (1,)), ((), ()))
    for h in range(ATTN_HEADS):
        sl = slice(h * dh, (h + 1) * dh)
        q = q_ref[:, sl]
        s_p = lax.dot_general(q, kp_ref[:, sl], nt, preferred_element_type=F32) + bias_ref[h, :, :blk]
        s_c = lax.dot_general(q, kc_ref[:, sl], nt, preferred_element_type=F32) + bias_ref[h, :, blk:]
        s_p = jnp.where(has_prev, s_p, MASK_VALUE)
        m = jnp.maximum(jnp.max(s_p, axis=1, keepdims=True), jnp.max(s_c, axis=1, keepdims=True))
        p_p = jnp.exp(s_p - m)
        p_c = jnp.exp(s_c - m)
        l = jnp.sum(p_p, axis=1, keepdims=True) + jnp.sum(p_c, axis=1, keepdims=True)
        o = (jnp.dot(p_p.astype(BF16), vp_ref[:, sl], preferred_element_type=F32)
             + jnp.dot(p_c.astype(BF16), vc_ref[:, sl], preferred_element_type=F32))
        o_ref[:, sl] = o / l
        lse_ref[:, sl] = jnp.broadcast_to(m + jnp.log(l), (blk, dh))


def _dilated_branch(aq, ak, av, bias, dilation, *, batch, seq):
    blk = ATTN_BLOCK
    w = ATTN_W
    sub_len = seq // dilation
    nb = sub_len // blk
    view = lambda t: t.reshape(batch * sub_len, dilation * w)
    cur = pl.BlockSpec((blk, w), lambda b, r, n: (b * nb + n, r))
    prev = pl.BlockSpec((blk, w), lambda b, r, n: (b * nb + jnp.maximum(n - 1, 0), r))
    o, lse = pl.pallas_call(
        _dilated_kernel,
        grid=(batch, dilation, nb),
        in_specs=[cur, prev, cur, prev, cur, _const_spec(bias.shape)],
        out_specs=[cur, cur],
        out_shape=[jax.ShapeDtypeStruct((batch * sub_len, dilation * w), F32)] * 2,
        compiler_params=_params(("arbitrary", "arbitrary", "arbitrary")),
        name=f"dilated_attn_d{dilation}",
    )(view(aq), view(ak), view(ak), view(av), view(av), bias)
    return o.reshape(batch * seq, w), lse.reshape(batch * seq, w)


def _ab_out_kernel(x_ref, mod_ref, g_ref, b_ref, ha_ref, o1_ref, o2_ref, o3_ref, l1_ref, l2_ref, l3_ref,
                   w_ref, out_ref, *, sub, alpha):
    l1, l2, l3 = l1_ref[...], l2_ref[...], l3_ref[...]
    mx = jnp.maximum(jnp.maximum(l1, l2), l3)
    e1, e2, e3 = jnp.exp(l1 - mx), jnp.exp(l2 - mx), jnp.exp(l3 - mx)
    hb = (e1 * o1_ref[...] + e2 * o2_ref[...] + e3 * o3_ref[...]) / (e1 + e2 + e3)
    wa = MLSTM_V_W
    y = (jnp.dot(ha_ref[...].astype(BF16), w_ref[:wa, :], preferred_element_type=F32)
         + jnp.dot(hb.astype(BF16), w_ref[wa:, :], preferred_element_type=F32))
    out_ref[...] = _post_norm(x_ref[...], y, mod_ref, sub, g_ref, b_ref, alpha, 1.0)


def _ab_output(x2, mod, ln_g, ln_b, ha, outs, lses, w_out, *, sub, alpha, seq):
    rows, d = x2.shape
    tiles_per_seq = seq // ROW_TILE
    row_spec = lambda w: pl.BlockSpec((ROW_TILE, w), lambda i: (i, 0))
    return pl.pallas_call(
        functools.partial(_ab_out_kernel, sub=sub, alpha=alpha),
        grid=(rows // ROW_TILE,),
        in_specs=[row_spec(d),
                  pl.BlockSpec((None, 9, d), lambda i: (i // tiles_per_seq, 0, 0)),
                  _const_spec((1, d)), _const_spec((1, d)),
                  row_spec(MLSTM_V_W)] + [row_spec(ATTN_W)] * 6 + [_const_spec(w_out.shape)],
        out_specs=row_spec(d),
        out_shape=jax.ShapeDtypeStruct((rows, d), F32),
        compiler_params=_params(("arbitrary",)),
        name="mixer_ab_out_proj",
    )(x2, mod, ln_g, ln_b, ha, *outs, *lses, w_out)


def _gelu_tanh(x):
    return 0.5 * x * (1.0 + jnp.tanh(math.sqrt(2.0 / math.pi) * (x + 0.044715 * (x * x * x))))


def _cd_kernel(x_ref, mod_ref, g_ref, b_ref, w_in_ref, w_out_ref, conv_w_ref, conv_b_ref,
               sgu_g_ref, sgu_b_ref, sgu_w_ref, sgu_bias_ref, out_ref, conv_ref, *, sub, alpha, tiles_per_seq):
    tm = x_ref.shape[0]
    halo = SUBLANES

    @pl.when(pl.program_id(0) % tiles_per_seq == 0)
    def _():
        conv_ref[0:halo, :] = jnp.zeros((halo, CONV_CH), F32)

    x = x_ref[...]
    h = _modulate(x, mod_ref, sub).astype(BF16)

    def proj(idx, width):
        return jnp.dot(h, w_in_ref[:, idx:idx + width], preferred_element_type=F32)

    gate_b = proj(0, CONV_CH)
    gate_c = proj(CONV_CH, CONV_CH)
    xc = proj(2 * CONV_CH, CONV_CH)
    u = proj(3 * CONV_CH, SGU_CH)
    v = proj(3 * CONV_CH + SGU_CH, SGU_CH)

    conv_ref[halo:halo + tm, :] = gate_c * xc
    conv = conv_b_ref[...]
    for j in range(CONV_K):
        start = halo - (CONV_K - 1) + j
        conv = conv + conv_w_ref[j:j + 1, :] * conv_ref[start:start + tm, :]
    conv_ref[0:halo, :] = conv_ref[tm:tm + halo, :]
    y_c = gate_b * conv

    u = _gelu_tanh(u)
    v = _gelu_tanh(v)
    row = lax.broadcasted_iota(jnp.int32, (SGU_CHUNK, SGU_CHUNK), 0)
    col = lax.broadcasted_iota(jnp.int32, (SGU_CHUNK, SGU_CHUNK), 1)
    y_d_groups = []
    for g in range(SGU_GROUPS):
        gs = slice(g * SGU_GROUP_CH, (g + 1) * SGU_GROUP_CH)
        vg = v[:, gs]
        mu = jnp.mean(vg, axis=-1, keepdims=True)
        vc = vg - mu
        var = jnp.mean(vc * vc, axis=-1, keepdims=True)
        vn = (vc * lax.rsqrt(var + LN_EPS) * sgu_g_ref[:, gs] + sgu_b_ref[:, gs]).astype(BF16)
        w_s = jnp.where(row >= col, sgu_w_ref[g], 0.0).astype(BF16)
        mixed = [jnp.dot(w_s, vn[t0:t0 + SGU_CHUNK, :], preferred_element_type=F32) + sgu_bias_ref[g]
                 for t0 in range(0, tm, SGU_CHUNK)]
        y_d_groups.append(u[:, gs] * jnp.concatenate(mixed, axis=0))
    y_d = jnp.concatenate(y_d_groups, axis=1)

    y = (jnp.dot(y_c.astype(BF16), w_out_ref[:CONV_CH, :], preferred_element_type=F32)
         + jnp.dot(y_d.astype(BF16), w_out_ref[CONV_CH:, :], preferred_element_type=F32))
    out_ref[...] = _post_norm(x, y, mod_ref, sub, g_ref, b_ref, alpha, 1.0)


def _cd_sublayer(x2, mod, ln_g, ln_b, w_in, w_out, conv_w, conv_b, sgu_g, sgu_b, sgu_w, sgu_bias,
                 *, sub, alpha, seq):
    rows, d = x2.shape
    tiles_per_seq = seq // ROW_TILE
    return pl.pallas_call(
        functools.partial(_cd_kernel, sub=sub, alpha=alpha, tiles_per_seq=tiles_per_seq),
        grid=(rows // ROW_TILE,),
        in_specs=[
            pl.BlockSpec((ROW_TILE, d), lambda i: (i, 0)),
            pl.BlockSpec((None, 9, d), lambda i: (i // tiles_per_seq, 0, 0)),
            _const_spec((1, d)), _const_spec((1, d)),
            _const_spec(w_in.shape), _const_spec(w_out.shape),
            _const_spec(conv_w.shape), _const_spec(conv_b.shape),
            _const_spec(sgu_g.shape), _const_spec(sgu_b.shape),
            _const_spec(sgu_w.shape), _const_spec(sgu_bias.shape),
        ],
        out_specs=pl.BlockSpec((ROW_TILE, d), lambda i: (i, 0)),
        out_shape=jax.ShapeDtypeStruct((rows, d), F32),
        scratch_shapes=[pltpu.VMEM((ROW_TILE + 2 * SUBLANES, CONV_CH), F32)],
        compiler_params=_params(("arbitrary",)),
        name="mixer_cd_sublayer",
    )(x2, mod, ln_g, ln_b, w_in, w_out, conv_w, conv_b, sgu_g, sgu_b, sgu_w, sgu_bias)


def kernel(x, c, rel_bias, ada_w, ada_b, ln_g, ln_b, ffn_w_gate, ffn_w_up, ffn_w_down, ab_w_in, ab_w_out,
           ab_b_igate, ab_b_fgate, cd_w_in, cd_w_out, cd_conv_w, cd_conv_b, cd_sgu_ln_g, cd_sgu_ln_b,
           cd_sgu_w, cd_sgu_b):
    batch, seq, d = x.shape
    depth = ada_w.shape[0]
    alpha = (2 * depth) ** 0.25
    assert seq % ROW_TILE == 0 and ROW_TILE % SGU_CHUNK == 0
    assert all(seq % (dil * ATTN_BLOCK) == 0 and win // dil == ATTN_BLOCK for win, dil in DILATED_BRANCHES)

    mods = _modulation(c, ada_w, ada_b)
    x2 = x.reshape(batch * seq, d)
    for layer in range(depth):
        mod = mods[layer]
        lg = ln_g[layer][:, None, :]
        lb = ln_b[layer][:, None, :]
        ffn = functools.partial(_ffn_sublayer, alpha=alpha, seq=seq)
        x2 = ffn(x2, mod, lg[0], lb[0], ffn_w_gate[layer, 0].astype(BF16), ffn_w_up[layer, 0].astype(BF16),
                 ffn_w_down[layer, 0].astype(BF16), sub=0)
        if layer % 2 == 0:
            e = layer // 2
            mq, mk, mv, mo, gates, aq, ak, av = _ab_projection(x2, mod, _pack_ab_w_in(ab_w_in[e]), sub=1, seq=seq)
            gate_bias = jnp.concatenate(
                [ab_b_igate[e], ab_b_fgate[e], jnp.zeros((GATE_W - 2 * MLSTM_HEADS,), F32)])[None, :]
            ha = _mlstm(mq, mk, mv, mo, gates, gate_bias, batch=batch, seq=seq)
            outs, lses = [], []
            for window, dilation in DILATED_BRANCHES:
                o, lse = _dilated_branch(aq, ak, av, _branch_bias(rel_bias, window, dilation), dilation,
                                         batch=batch, seq=seq)
                outs.append(o)
                lses.append(lse)
            x2 = _ab_output(x2, mod, lg[1], lb[1], ha, outs, lses, ab_w_out[e].astype(BF16),
                            sub=1, alpha=alpha, seq=seq)
        else:
            o = layer // 2
            sgu_bias = jnp.broadcast_to(cd_sgu_b[o][:, :, None], (SGU_GROUPS, SGU_CHUNK, SGU_GROUP_CH))
            x2 = _cd_sublayer(x2, mod, lg[1], lb[1], cd_w_in[o].astype(BF16), cd_w_out[o].astype(BF16),
                              cd_conv_w[o], cd_conv_b[o][None, :], cd_sgu_ln_g[o].reshape(1, SGU_CH),
                              cd_sgu_ln_b[o].reshape(1, SGU_CH), cd_sgu_w[o], sgu_bias,
                              sub=1, alpha=alpha, seq=seq)
        x2 = ffn(x2, mod, lg[2], lb[2], ffn_w_gate[layer, 1].astype(BF16), ffn_w_up[layer, 1].astype(BF16),
                 ffn_w_down[layer, 1].astype(BF16), sub=2)
    return x2.reshape(batch, seq, d)
```

```python
import functools
import math

import jax
import jax.numpy as jnp
import numpy as np
from jax import lax
from jax.experimental import pallas as pl
from jax.experimental.pallas import tpu as pltpu

F32 = jnp.float32
BF16 = jnp.bfloat16

MLSTM_HEADS = 4
MLSTM_QK_DIM = 64
MLSTM_V_DIM = 128
MLSTM_CHUNK = 128
ATTN_HEADS = 8
ATTN_HEAD_DIM = 64
DILATED_BRANCHES = ((128, 1), (512, 4), (2048, 16))
ATTN_BLOCK = 128
REL_BUCKETS = 32
REL_MAX_DIST = 2048
CONV_CH = 512
CONV_K = 3
SGU_CH = 512
SGU_GROUPS = 4
SGU_GROUP_CH = SGU_CH // SGU_GROUPS
SGU_CHUNK = 128
FFN_RES_W = 0.5
LN_EPS = 1e-5
MASK_VALUE = -1e30

MLSTM_QK_W = MLSTM_HEADS * MLSTM_QK_DIM
MLSTM_V_W = MLSTM_HEADS * MLSTM_V_DIM
ATTN_W = ATTN_HEADS * ATTN_HEAD_DIM
GATE_W = 128

LANES = 128
SUBLANES = 8
VMEM_LIMIT_BYTES = 56 * 1024 * 1024

ROW_TILE = 512
FF_CHUNK = 256


def _const_spec(shape):
    nd = len(shape)
    return pl.BlockSpec(shape, lambda *_: (0,) * nd, pipeline_mode=pl.Buffered(1))


def _params(sem):
    return pltpu.CompilerParams(dimension_semantics=sem, vmem_limit_bytes=VMEM_LIMIT_BYTES)


def _modulate(x, mod_ref, sub):
    shift = mod_ref[3 * sub:3 * sub + 1, :]
    scale = mod_ref[3 * sub + 1:3 * sub + 2, :]
    return x * (1.0 + scale) + shift


def _post_norm(x, y, mod_ref, sub, g_ref, b_ref, alpha, res_w):
    gate = mod_ref[3 * sub + 2:3 * sub + 3, :]
    r = alpha * x + res_w * (1.0 + gate) * y
    mu = jnp.mean(r, axis=-1, keepdims=True)
    rc = r - mu
    var = jnp.mean(rc * rc, axis=-1, keepdims=True)
    return rc * lax.rsqrt(var + LN_EPS) * g_ref[...] + b_ref[...]


def _mod_kernel(c_ref, w_ref, b_ref, o_ref):
    c = c_ref[...]
    sc = c * jax.nn.sigmoid(c)
    o_ref[...] = jnp.dot(sc, w_ref[...], preferred_element_type=F32) + b_ref[...]


def _modulation(c, ada_w, ada_b):
    depth, d, n = ada_w.shape
    b = c.shape[0]
    bp = -(-b // SUBLANES) * SUBLANES
    cp = jnp.pad(c, ((0, bp - b), (0, 0)))
    bn = 1024
    out = pl.pallas_call(
        _mod_kernel,
        grid=(depth, n // bn),
        in_specs=[
            pl.BlockSpec((bp, d), lambda l, j: (0, 0)),
            pl.BlockSpec((None, d, bn), lambda l, j: (l, 0, j)),
            pl.BlockSpec((None, 1, bn), lambda l, j: (l, 0, j)),
        ],
        out_specs=pl.BlockSpec((None, bp, bn), lambda l, j: (l, 0, j)),
        out_shape=jax.ShapeDtypeStruct((depth, bp, n), F32),
        compiler_params=_params(("arbitrary", "arbitrary")),
        name="adaln_mod",
    )(cp, ada_w, ada_b.reshape(depth, 1, n))
    return out[:, :b].reshape(depth, b, 9, d)


def _ffn_kernel(x_ref, mod_ref, g_ref, b_ref, wg_ref, wu_ref, wd_ref, o_ref, *, sub, alpha):
    x = x_ref[...]
    h = _modulate(x, mod_ref, sub).astype(BF16)
    d_ff = wg_ref.shape[1]
    acc = jnp.zeros(x.shape, F32)
    for c0 in range(0, d_ff, FF_CHUNK):
        g = jnp.dot(h, wg_ref[:, c0:c0 + FF_CHUNK], preferred_element_type=F32)
        u = jnp.dot(h, wu_ref[:, c0:c0 + FF_CHUNK], preferred_element_type=F32)
        a = (g * jax.nn.sigmoid(g) * u).astype(BF16)
        acc = acc + jnp.dot(a, wd_ref[c0:c0 + FF_CHUNK, :], preferred_element_type=F32)
    o_ref[...] = _post_norm(x, acc, mod_ref, sub, g_ref, b_ref, alpha, FFN_RES_W)


def _ffn_sublayer(x2, mod, ln_g, ln_b, wg, wu, wd, *, sub, alpha, seq):
    rows, d = x2.shape
    d_ff = wg.shape[1]
    tiles_per_seq = seq // ROW_TILE
    return pl.pallas_call(
        functools.partial(_ffn_kernel, sub=sub, alpha=alpha),
        grid=(rows // ROW_TILE,),
        in_specs=[
            pl.BlockSpec((ROW_TILE, d), lambda i: (i, 0)),
            pl.BlockSpec((None, 9, d), lambda i: (i // tiles_per_seq, 0, 0)),
            _const_spec((1, d)),
            _const_spec((1, d)),
            _const_spec((d, d_ff)),
            _const_spec((d, d_ff)),
            _const_spec((d_ff, d)),
        ],
        out_specs=pl.BlockSpec((ROW_TILE, d), lambda i: (i, 0)),
        out_shape=jax.ShapeDtypeStruct((rows, d), F32),
        compiler_params=_params(("arbitrary",)),
        name="swiglu_sublayer",
    )(x2, mod, ln_g, ln_b, wg, wu, wd)


_AB_SEGS = (
    ("mq", MLSTM_QK_W, BF16, 1.0),
    ("mk", MLSTM_QK_W, BF16, MLSTM_QK_DIM ** -0.5),
    ("mv", MLSTM_V_W, BF16, 1.0),
    ("mo", MLSTM_V_W, F32, 1.0),
    ("gates", GATE_W, F32, 1.0),
    ("aq", ATTN_W, F32, ATTN_HEAD_DIM ** -0.5),
    ("ak", ATTN_W, F32, 1.0),
    ("av", ATTN_W, F32, 1.0),
)
_AB_PACKED_W = sum(s[1] for s in _AB_SEGS)
_AB_SLAB_SEGS = ("aq", "ak", "av")
ATTN_SLABS = ATTN_W // LANES


def _pack_ab_w_in(w_in):
    d = w_in.shape[0]
    sizes = (MLSTM_QK_W, MLSTM_QK_W, MLSTM_V_W, MLSTM_V_W, MLSTM_HEADS, MLSTM_HEADS, ATTN_W, ATTN_W, ATTN_W)
    mq, mk, mv, mo, mi, mf, aq, ak, av = jnp.split(w_in, np.cumsum(sizes)[:-1].tolist(), axis=1)
    gates = jnp.concatenate([mi, mf, jnp.zeros((d, GATE_W - 2 * MLSTM_HEADS), w_in.dtype)], axis=1)
    return jnp.concatenate([mq, mk, mv, mo, gates, aq, ak, av], axis=1).astype(BF16)


def _ab_proj_kernel(x_ref, mod_ref, w_ref, *out_refs, sub):
    h = _modulate(x_ref[...], mod_ref, sub).astype(BF16)
    off = 0
    for (name, width, dtype, scale), o_ref in zip(_AB_SEGS, out_refs):
        z = jnp.dot(h, w_ref[:, off:off + width], preferred_element_type=F32)
        if scale != 1.0:
            z = z * scale
        if name in _AB_SLAB_SEGS:
            for s in range(ATTN_SLABS):
                o_ref[s] = z[:, s * LANES:(s + 1) * LANES]
        else:
            o_ref[...] = z.astype(dtype)
        off += width


def _ab_projection(x2, mod, w_packed, *, sub, seq):
    rows, d = x2.shape
    tiles_per_seq = seq // ROW_TILE
    out_specs, out_shape = [], []
    for name, width, dtype, _ in _AB_SEGS:
        if name in _AB_SLAB_SEGS:
            out_specs.append(pl.BlockSpec((ATTN_SLABS, ROW_TILE, LANES), lambda i: (0, i, 0)))
            out_shape.append(jax.ShapeDtypeStruct((ATTN_SLABS, rows, LANES), dtype))
        else:
            out_specs.append(pl.BlockSpec((ROW_TILE, width), lambda i: (i, 0)))
            out_shape.append(jax.ShapeDtypeStruct((rows, width), dtype))
    return pl.pallas_call(
        functools.partial(_ab_proj_kernel, sub=sub),
        grid=(rows // ROW_TILE,),
        in_specs=[
            pl.BlockSpec((ROW_TILE, d), lambda i: (i, 0)),
            pl.BlockSpec((None, 9, d), lambda i: (i // tiles_per_seq, 0, 0)),
            _const_spec((d, _AB_PACKED_W)),
        ],
        out_specs=out_specs,
        out_shape=out_shape,
        compiler_params=_params(("arbitrary",)),
        name="mixer_ab_in_proj",
    )(x2, mod, w_packed)


def _mlstm_kernel(q_ref, k_ref, v_ref, og_ref, gates_ref, gbias_ref, out_ref, c_ref, n_ref, m_ref):
    L = MLSTM_CHUNK
    H = MLSTM_HEADS
    DK = MLSTM_QK_DIM
    DV = MLSTM_V_DIM

    @pl.when(pl.program_id(1) == 0)
    def _():
        c_ref[...] = jnp.zeros_like(c_ref)
        n_ref[...] = jnp.zeros_like(n_ref)
        m_ref[...] = jnp.zeros_like(m_ref)

    row = lax.broadcasted_iota(jnp.int32, (L, L), 0)
    col = lax.broadcasted_iota(jnp.int32, (L, L), 1)
    causal = row >= col

    gb = gates_ref[...] + gbias_ref[...]
    log_f = -(jnp.maximum(-gb, 0.0) + jnp.log1p(jnp.exp(-jnp.abs(gb))))
    log_f = jnp.where((col >= H) & (col < 2 * H), log_f, 0.0)
    cum_f = jnp.dot(causal.astype(F32), log_f, precision=lax.Precision.HIGHEST,
                    preferred_element_type=F32)
    a = jnp.where(col < H, gb, cum_f)
    a_t = a.T

    w_cols = []
    decays = []
    m_prevs = []
    for h in range(H):
        li_col = a[:, h:h + 1]
        cf_col = a[:, H + h:H + h + 1]
        chunk_f = cf_col[L - 1:L, :]
        m_prev = m_ref[h][0:1, 0:1]
        g_col = chunk_f - cf_col + li_col
        g_max = jnp.max(g_col, axis=0, keepdims=True)
        m_new = jnp.maximum(chunk_f + m_prev, g_max)
        decays.append(jnp.exp(chunk_f + m_prev - m_new))
        w_cols.append(jnp.broadcast_to(jnp.exp(g_col - m_new), (L, DK)))
        m_prevs.append(m_prev)
        m_ref[h] = jnp.broadcast_to(m_new, m_ref.shape[1:])
    kw = k_ref[...].astype(F32) * jnp.concatenate(w_cols, axis=1)
    kw_t = kw.T

    for h in range(H):
        q = q_ref[:, h * DK:(h + 1) * DK]
        k = k_ref[:, h * DK:(h + 1) * DK]
        v = v_ref[:, h * DV:(h + 1) * DV]
        li_row = a_t[h:h + 1, :]
        cf_row = a_t[H + h:H + h + 1, :]
        cf_col = a[:, H + h:H + h + 1]
        m_prev = m_prevs[h]
        c_prev = c_ref[h]
        n_prev = n_ref[h]

        d_log = jnp.where(causal, cf_col - cf_row + li_row, -jnp.inf)
        inter_log = cf_col + m_prev
        m_t = jnp.maximum(inter_log, jnp.max(d_log, axis=1, keepdims=True))
        s = lax.dot_general(q, k, (((1,), (1,)), ((), ())), preferred_element_type=F32)
        w_intra = jnp.exp(d_log - m_t) * s
        w_inter = jnp.exp(inter_log - m_t)
        num = (jnp.dot(w_intra.astype(BF16), v, preferred_element_type=F32)
               + jnp.dot(q, c_prev.astype(BF16), preferred_element_type=F32) * w_inter)
        den = (jnp.sum(w_intra, axis=1, keepdims=True)
               + jnp.sum(q.astype(F32) * n_prev, axis=1, keepdims=True) * w_inter)
        den = jnp.maximum(jnp.abs(den), jnp.exp(-m_t))
        o_gate = jax.nn.sigmoid(og_ref[:, h * DV:(h + 1) * DV])
        out_ref[:, h * DV:(h + 1) * DV] = o_gate * (num / den)

        kw_t_h = kw_t[h * DK:(h + 1) * DK, :]
        c_ref[h] = decays[h] * c_prev + jnp.dot(kw_t_h.astype(BF16), v, preferred_element_type=F32)
        n_ref[h] = decays[h] * n_prev + jnp.sum(kw[:, h * DK:(h + 1) * DK], axis=0, keepdims=True)


def _mlstm(mq, mk, mv, mo, gates, gate_bias, *, batch, seq):
    L = MLSTM_CHUNK
    nc = seq // L

    def spec(w):
        return pl.BlockSpec((L, w), lambda b, c: (b * nc + c, 0))

    return pl.pallas_call(
        _mlstm_kernel,
        grid=(batch, nc),
        in_specs=[spec(MLSTM_QK_W), spec(MLSTM_QK_W), spec(MLSTM_V_W), spec(MLSTM_V_W), spec(GATE_W),
                  pl.BlockSpec((1, GATE_W), lambda b, c: (0, 0))],
        out_specs=spec(MLSTM_V_W),
        out_shape=jax.ShapeDtypeStruct((batch * seq, MLSTM_V_W), F32),
        scratch_shapes=[
            pltpu.VMEM((MLSTM_HEADS, MLSTM_QK_DIM, MLSTM_V_DIM), F32),
            pltpu.VMEM((MLSTM_HEADS, 1, MLSTM_QK_DIM), F32),
            pltpu.VMEM((MLSTM_HEADS, SUBLANES, LANES), F32),
        ],
        compiler_params=_params(("arbitrary", "arbitrary")),
        name="mlstm_chunkwise",
    )(mq, mk, mv, mo, gates, gate_bias)


def _rel_bucket(dist):
    max_exact = REL_BUCKETS // 2
    d = jnp.maximum(dist, 0)
    large = max_exact + (jnp.log(jnp.maximum(d, 1).astype(F32) / max_exact)
                         / math.log(REL_MAX_DIST / max_exact) * (REL_BUCKETS - max_exact)).astype(jnp.int32)
    large = jnp.minimum(large, REL_BUCKETS - 1)
    return jnp.where(d < max_exact, d, large)


def _branch_bias(rel_bias, window, dilation):
    blk = ATTN_BLOCK
    span = window // dilation
    qi = jnp.arange(blk)[:, None]
    kj = jnp.arange(2 * blk)[None, :]
    dist = qi + blk - kj
    valid = (dist >= 0) & (dist <= span)
    one_hot = (_rel_bucket(dist * dilation)[..., None] == jnp.arange(REL_BUCKETS)).astype(F32)
    bias = jnp.einsum('qkb,bh->hqk', one_hot, rel_bias.astype(F32), precision=lax.Precision.HIGHEST)
    return jnp.where(valid[None], bias, MASK_VALUE)


ATTN_GROUP = ATTN_BLOCK * max(d for _, d in DILATED_BRANCHES)
HEADS_PER_SLAB = LANES // ATTN_HEAD_DIM


def _attn_kernel(q_ref, k_ref, v_ref, bias_ref, out_ref, k_s, v_s, m_s, l_s, acc_s):
    blk = ATTN_BLOCK
    grp = ATTN_GROUP
    first_group = pl.program_id(2) == 0

    @pl.when(first_group)
    def _():
        k_s[0:grp, :] = jnp.zeros((grp, LANES), F32)
        v_s[0:grp, :] = jnp.zeros((grp, LANES), F32)

    @pl.when(jnp.logical_not(first_group))
    def _():
        k_s[0:grp, :] = k_s[grp:2 * grp, :]
        v_s[0:grp, :] = v_s[grp:2 * grp, :]

    k_s[grp:2 * grp, :] = k_ref[...]
    v_s[grp:2 * grp, :] = v_ref[...]

    head_a = lax.broadcasted_iota(jnp.int32, (blk, LANES), 1) < ATTN_HEAD_DIM
    no_prev = jnp.where(first_group, MASK_VALUE, 0.0)
    nt = (((1,), (1,)), ((), ()))

    def unit(branch, d, start, *, first, last, prev_in_prev_group):
        def rows(base):
            return pl.ds(base, blk, stride=d) if d > 1 else pl.ds(base, blk)

        q = q_ref[rows(start), :]
        q2 = jnp.concatenate([jnp.where(head_a, q, 0.0), jnp.where(head_a, 0.0, q)], axis=0).astype(BF16)
        prev_rows, own_rows = rows(grp + start - blk * d), rows(grp + start)
        kpc = jnp.concatenate([k_s[prev_rows, :], k_s[own_rows, :]], axis=0).astype(BF16)
        vpc = jnp.concatenate([v_s[prev_rows, :], v_s[own_rows, :]], axis=0).astype(BF16)
        s = lax.dot_general(q2, kpc, nt, preferred_element_type=F32) + bias_ref[branch]
        if prev_in_prev_group:
            s = jnp.concatenate([s[:, :blk] + no_prev, s[:, blk:]], axis=1)
        ps, ls, alphas = [], [], []
        for hh in range(HEADS_PER_SLAB):
            sh = s[hh * blk:(hh + 1) * blk, :]
            m_new = jnp.broadcast_to(jnp.max(sh, axis=1, keepdims=True), (blk, LANES))
            if not first:
                m_old = m_s[hh, rows(start), :]
                m_new = jnp.maximum(m_old, m_new)
            p = jnp.exp(sh - jnp.concatenate([m_new, m_new], axis=1))
            l_new = jnp.broadcast_to(jnp.sum(p, axis=1, keepdims=True), (blk, LANES))
            if not first:
                alpha = jnp.exp(m_old - m_new)
                l_new = alpha * l_s[hh, rows(start), :] + l_new
                alphas.append(alpha)
            if not last:
                m_s[hh, rows(start), :] = m_new
                l_s[hh, rows(start), :] = l_new
            ps.append(p)
            ls.append(l_new)
        pv = jnp.dot(jnp.concatenate(ps, axis=0).astype(BF16), vpc, preferred_element_type=F32)
        acc = jnp.where(head_a, pv[:blk], pv[blk:])
        if not first:
            acc = jnp.where(head_a, alphas[0], alphas[1]) * acc_s[rows(start), :] + acc
        if last:
            out_ref[rows(start), :] = acc / jnp.where(head_a, ls[0], ls[1])
        else:
            acc_s[rows(start), :] = acc

    (_, d_near), (_, d_mid), (_, d_far) = DILATED_BRANCHES

    def far_body(i, carry):
        for u in range(2):
            unit(2, d_far, i * 2 + u, first=True, last=False, prev_in_prev_group=True)
        return carry

    lax.fori_loop(0, d_far // 2, far_body, 0)

    blocks_mid = grp // (d_mid * blk)

    def mid_body(r, carry):
        for j in range(blocks_mid):
            unit(1, d_mid, r + j * d_mid * blk, first=False, last=False, prev_in_prev_group=(j == 0))
        return carry

    lax.fori_loop(0, d_mid, mid_body, 0)

    unit(0, d_near, 0, first=False, last=True, prev_in_prev_group=True)

    def near_body(i, carry):
        for u in range(3):
            start = pl.multiple_of((1 + i * 3 + u) * blk, blk)
            unit(0, d_near, start, first=False, last=True, prev_in_prev_group=False)
        return carry

    lax.fori_loop(0, (grp // blk - 1) // 3, near_body, 0)


def _dilated_attention(aq, ak, av, rel_bias, *, batch, seq):
    grp = ATTN_GROUP
    blk = ATTN_BLOCK
    assert [d for _, d in DILATED_BRANCHES] == [1, 4, 16] and (grp // blk - 1) % 3 == 0
    groups = seq // grp
    rows = batch * seq
    bias = jnp.stack([_branch_bias(rel_bias, w, d).reshape(ATTN_SLABS, HEADS_PER_SLAB * blk, 2 * blk)
                      for w, d in DILATED_BRANCHES])
    slab = pl.BlockSpec((None, grp, LANES), lambda s, b, g: (s, b * groups + g, 0))
    return pl.pallas_call(
        _attn_kernel,
        grid=(ATTN_SLABS, batch, groups),
        in_specs=[slab, slab, slab,
                  pl.BlockSpec((len(DILATED_BRANCHES), None, HEADS_PER_SLAB * blk, 2 * blk),
                               lambda s, b, g: (0, s, 0, 0))],
        out_specs=slab,
        out_shape=jax.ShapeDtypeStruct((ATTN_SLABS, rows, LANES), F32),
        scratch_shapes=[
            pltpu.VMEM((2 * grp, LANES), F32),
            pltpu.VMEM((2 * grp, LANES), F32),
            pltpu.VMEM((HEADS_PER_SLAB, grp, LANES), F32),
            pltpu.VMEM((HEADS_PER_SLAB, grp, LANES), F32),
            pltpu.VMEM((grp, LANES), F32),
        ],
        compiler_params=_params(("arbitrary", "arbitrary", "arbitrary")),
        name="dilated_attention",
    )(aq, ak, av, bias)


def _ab_out_kernel(x_ref, mod_ref, g_ref, b_ref, ha_ref, hb_ref, w_ref, out_ref, *, sub, alpha):
    hb = jnp.concatenate([hb_ref[s] for s in range(ATTN_SLABS)], axis=1)
    wa = MLSTM_V_W
    y = (jnp.dot(ha_ref[...].astype(BF16), w_ref[:wa, :], preferred_element_type=F32)
         + jnp.dot(hb.astype(BF16), w_ref[wa:, :], preferred_element_type=F32))
    out_ref[...] = _post_norm(x_ref[...], y, mod_ref, sub, g_ref, b_ref, alpha, 1.0)


def _ab_output(x2, mod, ln_g, ln_b, ha, hb, w_out, *, sub, alpha, seq):
    rows, d = x2.shape
    tiles_per_seq = seq // ROW_TILE
    row_spec = lambda w: pl.BlockSpec((ROW_TILE, w), lambda i: (i, 0))
    return pl.pallas_call(
        functools.partial(_ab_out_kernel, sub=sub, alpha=alpha),
        grid=(rows // ROW_TILE,),
        in_specs=[row_spec(d),
                  pl.BlockSpec((None, 9, d), lambda i: (i // tiles_per_seq, 0, 0)),
                  _const_spec((1, d)), _const_spec((1, d)),
                  row_spec(MLSTM_V_W),
                  pl.BlockSpec((ATTN_SLABS, ROW_TILE, LANES), lambda i: (0, i, 0)),
                  _const_spec(w_out.shape)],
        out_specs=row_spec(d),
        out_shape=jax.ShapeDtypeStruct((rows, d), F32),
        compiler_params=_params(("arbitrary",)),
        name="mixer_ab_out_proj",
    )(x2, mod, ln_g, ln_b, ha, hb, w_out)


def _gelu_tanh(x):
    return 0.5 * x * (1.0 + jnp.tanh(math.sqrt(2.0 / math.pi) * (x + 0.044715 * (x * x * x))))


def _cd_kernel(x_ref, mod_ref, g_ref, b_ref, w_in_ref, w_out_ref, conv_w_ref, conv_b_ref,
               sgu_g_ref, sgu_b_ref, sgu_w_ref, sgu_bias_ref, out_ref, conv_ref, *, sub, alpha, tiles_per_seq):
    tm = x_ref.shape[0]
    halo = SUBLANES

    @pl.when(pl.program_id(0) % tiles_per_seq == 0)
    def _():
        conv_ref[0:halo, :] = jnp.zeros((halo, CONV_CH), F32)

    x = x_ref[...]
    h = _modulate(x, mod_ref, sub).astype(BF16)

    def proj(idx, width):
        return jnp.dot(h, w_in_ref[:, idx:idx + width], preferred_element_type=F32)

    gate_b = proj(0, CONV_CH)
    gate_c = proj(CONV_CH, CONV_CH)
    xc = proj(2 * CONV_CH, CONV_CH)
    u = proj(3 * CONV_CH, SGU_CH)
    v = proj(3 * CONV_CH + SGU_CH, SGU_CH)

    conv_ref[halo:halo + tm, :] = gate_c * xc
    conv = conv_b_ref[...]
    for j in range(CONV_K):
        start = halo - (CONV_K - 1) + j
        conv = conv + conv_w_ref[j:j + 1, :] * conv_ref[start:start + tm, :]
    conv_ref[0:halo, :] = conv_ref[tm:tm + halo, :]
    y_c = gate_b * conv

    u = _gelu_tanh(u)
    v = _gelu_tanh(v)
    row = lax.broadcasted_iota(jnp.int32, (SGU_CHUNK, SGU_CHUNK), 0)
    col = lax.broadcasted_iota(jnp.int32, (SGU_CHUNK, SGU_CHUNK), 1)
    y_d_groups = []
    for g in range(SGU_GROUPS):
        gs = slice(g * SGU_GROUP_CH, (g + 1) * SGU_GROUP_CH)
        vg = v[:, gs]
        mu = jnp.mean(vg, axis=-1, keepdims=True)
        vc = vg - mu
        var = jnp.mean(vc * vc, axis=-1, keepdims=True)
        vn = (vc * lax.rsqrt(var + LN_EPS) * sgu_g_ref[:, gs] + sgu_b_ref[:, gs]).astype(BF16)
        w_s = jnp.where(row >= col, sgu_w_ref[g], 0.0).astype(BF16)
        mixed = [jnp.dot(w_s, vn[t0:t0 + SGU_CHUNK, :], preferred_element_type=F32) + sgu_bias_ref[g]
                 for t0 in range(0, tm, SGU_CHUNK)]
        y_d_groups.append(u[:, gs] * jnp.concatenate(mixed, axis=0))
    y_d = jnp.concatenate(y_d_groups, axis=1)

    y = (jnp.dot(y_c.astype(BF16), w_out_ref[:CONV_CH, :], preferred_element_type=F32)
         + jnp.dot(y_d.astype(BF16), w_out_ref[CONV_CH:, :], preferred_element_type=F32))
    out_ref[...] = _post_norm(x, y, mod_ref, sub, g_ref, b_ref, alpha, 1.0)


def _cd_sublayer(x2, mod, ln_g, ln_b, w_in, w_out, conv_w, conv_b, sgu_g, sgu_b, sgu_w, sgu_bias,
                 *, sub, alpha, seq):
    rows, d = x2.shape
    tiles_per_seq = seq // ROW_TILE
    return pl.pallas_call(
        functools.partial(_cd_kernel, sub=sub, alpha=alpha, tiles_per_seq=tiles_per_seq),
        grid=(rows // ROW_TILE,),
        in_specs=[
            pl.BlockSpec((ROW_TILE, d), lambda i: (i, 0)),
            pl.BlockSpec((None, 9, d), lambda i: (i // tiles_per_seq, 0, 0)),
            _const_spec((1, d)), _const_spec((1, d)),
            _const_spec(w_in.shape), _const_spec(w_out.shape),
            _const_spec(conv_w.shape), _const_spec(conv_b.shape),
            _const_spec(sgu_g.shape), _const_spec(sgu_b.shape),
            _const_spec(sgu_w.shape), _const_spec(sgu_bias.shape),
        ],
        out_specs=pl.BlockSpec((ROW_TILE, d), lambda i: (i, 0)),
        out_shape=jax.ShapeDtypeStruct((rows, d), F32),
        scratch_shapes=[pltpu.VMEM((ROW_TILE + 2 * SUBLANES, CONV_CH), F32)],
        compiler_params=_params(("arbitrary",)),
        name="mixer_cd_sublayer",
    )(x2, mod, ln_g, ln_b, w_in, w_out, conv_w, conv_b, sgu_g, sgu_b, sgu_w, sgu_bias)


def kernel(x, c, rel_bias, ada_w, ada_b, ln_g, ln_b, ffn_w_gate, ffn_w_up, ffn_w_down, ab_w_in, ab_w_out,
           ab_b_igate, ab_b_fgate, cd_w_in, cd_w_out, cd_conv_w, cd_conv_b, cd_sgu_ln_g, cd_sgu_ln_b,
           cd_sgu_w, cd_sgu_b):
    batch, seq, d = x.shape
    depth = ada_w.shape[0]
    alpha = (2 * depth) ** 0.25
    assert seq % ROW_TILE == 0 and ROW_TILE % SGU_CHUNK == 0
    assert all(seq % (dil * ATTN_BLOCK) == 0 and win // dil == ATTN_BLOCK for win, dil in DILATED_BRANCHES)

    mods = _modulation(c, ada_w, ada_b)
    x2 = x.reshape(batch * seq, d)
    for layer in range(depth):
        mod = mods[layer]
        lg = ln_g[layer][:, None, :]
        lb = ln_b[layer][:, None, :]
        ffn = functools.partial(_ffn_sublayer, alpha=alpha, seq=seq)
        x2 = ffn(x2, mod, lg[0], lb[0], ffn_w_gate[layer, 0].astype(BF16), ffn_w_up[layer, 0].astype(BF16),
                 ffn_w_down[layer, 0].astype(BF16), sub=0)
        if layer % 2 == 0:
            e = layer // 2
            mq, mk, mv, mo, gates, aq, ak, av = _ab_projection(x2, mod, _pack_ab_w_in(ab_w_in[e]), sub=1, seq=seq)
            gate_bias = jnp.concatenate(
                [ab_b_igate[e], ab_b_fgate[e], jnp.zeros((GATE_W - 2 * MLSTM_HEADS,), F32)])[None, :]
            ha = _mlstm(mq, mk, mv, mo, gates, gate_bias, batch=batch, seq=seq)
            hb = _dilated_attention(aq, ak, av, rel_bias, batch=batch, seq=seq)
            x2 = _ab_output(x2, mod, lg[1], lb[1], ha, hb, ab_w_out[e].astype(BF16),
                            sub=1, alpha=alpha, seq=seq)
        else:
            o = layer // 2
            sgu_bias = jnp.broadcast_to(cd_sgu_b[o][:, :, None], (SGU_GROUPS, SGU_CHUNK, SGU_GROUP_CH))
            x2 = _cd_sublayer(x2, mod, lg[1], lb[1], cd_w_in[o].astype(BF16), cd_w_out[o].astype(BF16),
                              cd_conv_w[o], cd_conv_b[o][None, :], cd_sgu_ln_g[o].reshape(1, SGU_CH),
                              cd_sgu_ln_b[o].reshape(1, SGU_CH), cd_sgu_w[o], sgu_bias,
                              sub=1, alpha=alpha, seq=seq)
        x2 = ffn(x2, mod, lg[2], lb[2], ffn_w_gate[layer, 1].astype(BF16), ffn_w_up[layer, 1].astype(BF16),
                 ffn_w_down[layer, 1].astype(BF16), sub=2)
    return x2.reshape(batch, seq, d)
```

```python
import functools
import math

import jax
import jax.numpy as jnp
import numpy as np
from jax import lax
from jax.experimental import pallas as pl
from jax.experimental.pallas import tpu as pltpu

F32 = jnp.float32
BF16 = jnp.bfloat16

MLSTM_HEADS = 4
MLSTM_QK_DIM = 64
MLSTM_V_DIM = 128
MLSTM_CHUNK = 128
ATTN_HEADS = 8
ATTN_HEAD_DIM = 64
DILATED_BRANCHES = ((128, 1), (512, 4), (2048, 16))
ATTN_BLOCK = 128
REL_BUCKETS = 32
REL_MAX_DIST = 2048
CONV_CH = 512
CONV_K = 3
SGU_CH = 512
SGU_GROUPS = 4
SGU_GROUP_CH = SGU_CH // SGU_GROUPS
SGU_CHUNK = 128
FFN_RES_W = 0.5
LN_EPS = 1e-5
MASK_VALUE = -1e30
LOG2_E = math.log2(math.e)

MLSTM_QK_W = MLSTM_HEADS * MLSTM_QK_DIM
MLSTM_V_W = MLSTM_HEADS * MLSTM_V_DIM
ATTN_W = ATTN_HEADS * ATTN_HEAD_DIM
GATE_W = 128

LANES = 128
SUBLANES = 8
VMEM_LIMIT_BYTES = 56 * 1024 * 1024

ROW_TILE = 512
FF_CHUNK = 256


def _const_spec(shape):
    nd = len(shape)
    return pl.BlockSpec(shape, lambda *_: (0,) * nd, pipeline_mode=pl.Buffered(1))


def _params(sem):
    return pltpu.CompilerParams(dimension_semantics=sem, vmem_limit_bytes=VMEM_LIMIT_BYTES)


def _modulate(x, mod_ref, sub):
    shift = mod_ref[3 * sub:3 * sub + 1, :]
    scale = mod_ref[3 * sub + 1:3 * sub + 2, :]
    return x * (1.0 + scale) + shift


def _post_norm(x, y, mod_ref, sub, g_ref, b_ref, alpha, res_w):
    gate = mod_ref[3 * sub + 2:3 * sub + 3, :]
    r = alpha * x + res_w * (1.0 + gate) * y
    mu = jnp.mean(r, axis=-1, keepdims=True)
    rc = r - mu
    var = jnp.mean(rc * rc, axis=-1, keepdims=True)
    return rc * lax.rsqrt(var + LN_EPS) * g_ref[...] + b_ref[...]


def _mod_kernel(c_ref, w_ref, b_ref, o_ref):
    c = c_ref[...]
    sc = c * jax.nn.sigmoid(c)
    o_ref[...] = jnp.dot(sc, w_ref[...], preferred_element_type=F32) + b_ref[...]


def _modulation(c, ada_w, ada_b):
    depth, d, n = ada_w.shape
    b = c.shape[0]
    bp = -(-b // SUBLANES) * SUBLANES
    cp = jnp.pad(c, ((0, bp - b), (0, 0)))
    bn = 1024
    out = pl.pallas_call(
        _mod_kernel,
        grid=(depth, n // bn),
        in_specs=[
            pl.BlockSpec((bp, d), lambda l, j: (0, 0)),
            pl.BlockSpec((None, d, bn), lambda l, j: (l, 0, j)),
            pl.BlockSpec((None, 1, bn), lambda l, j: (l, 0, j)),
        ],
        out_specs=pl.BlockSpec((None, bp, bn), lambda l, j: (l, 0, j)),
        out_shape=jax.ShapeDtypeStruct((depth, bp, n), F32),
        compiler_params=_params(("arbitrary", "arbitrary")),
        name="adaln_mod",
    )(cp, ada_w, ada_b.reshape(depth, 1, n))
    return out[:, :b].reshape(depth, b, 9, d)


def _ffn_kernel(x_ref, mod_ref, g_ref, b_ref, wg_ref, wu_ref, wd_ref, o_ref, *, sub, alpha):
    x = x_ref[...]
    h = _modulate(x, mod_ref, sub).astype(BF16)
    d_ff = wg_ref.shape[1]
    acc = jnp.zeros(x.shape, F32)
    for c0 in range(0, d_ff, FF_CHUNK):
        g = jnp.dot(h, wg_ref[:, c0:c0 + FF_CHUNK].astype(BF16), preferred_element_type=F32)
        u = jnp.dot(h, wu_ref[:, c0:c0 + FF_CHUNK].astype(BF16), preferred_element_type=F32)
        a = (g * jax.nn.sigmoid(g) * u).astype(BF16)
        acc = acc + jnp.dot(a, wd_ref[c0:c0 + FF_CHUNK, :].astype(BF16), preferred_element_type=F32)
    o_ref[...] = _post_norm(x, acc, mod_ref, sub, g_ref, b_ref, alpha, FFN_RES_W)


def _ffn_sublayer(x2, mod, ln_g, ln_b, wg, wu, wd, *, layer, which, sub, alpha, seq):
    rows, d = x2.shape
    d_ff = wg.shape[-1]
    tiles_per_seq = seq // ROW_TILE

    def weight_spec(shape):
        return pl.BlockSpec((None, None) + shape, lambda i: (layer, which, 0, 0), pipeline_mode=pl.Buffered(1))

    return pl.pallas_call(
        functools.partial(_ffn_kernel, sub=sub, alpha=alpha),
        grid=(rows // ROW_TILE,),
        in_specs=[
            pl.BlockSpec((ROW_TILE, d), lambda i: (i, 0)),
            pl.BlockSpec((None, 9, d), lambda i: (i // tiles_per_seq, 0, 0)),
            _const_spec((1, d)),
            _const_spec((1, d)),
            weight_spec((d, d_ff)),
            weight_spec((d, d_ff)),
            weight_spec((d_ff, d)),
        ],
        out_specs=pl.BlockSpec((ROW_TILE, d), lambda i: (i, 0)),
        out_shape=jax.ShapeDtypeStruct((rows, d), F32),
        compiler_params=_params(("arbitrary",)),
        name="swiglu_sublayer",
    )(x2, mod, ln_g, ln_b, wg, wu, wd)


_AB_SEGS = (
    ("mq", MLSTM_QK_W, BF16, 1.0),
    ("mk", MLSTM_QK_W, BF16, MLSTM_QK_DIM ** -0.5),
    ("mv", MLSTM_V_W, BF16, 1.0),
    ("mo", MLSTM_V_W, F32, 1.0),
    ("gates", GATE_W, F32, 1.0),
    ("aq", ATTN_W, F32, ATTN_HEAD_DIM ** -0.5 * LOG2_E),
    ("ak", ATTN_W, F32, 1.0),
    ("av", ATTN_W, F32, 1.0),
)
_AB_PACKED_W = sum(s[1] for s in _AB_SEGS)
_AB_SLAB_SEGS = ("aq", "ak", "av")
ATTN_SLABS = ATTN_W // LANES


def _pack_ab_w_in(w_in):
    d = w_in.shape[0]
    sizes = (MLSTM_QK_W, MLSTM_QK_W, MLSTM_V_W, MLSTM_V_W, MLSTM_HEADS, MLSTM_HEADS, ATTN_W, ATTN_W, ATTN_W)
    mq, mk, mv, mo, mi, mf, aq, ak, av = jnp.split(w_in, np.cumsum(sizes)[:-1].tolist(), axis=1)
    gates = jnp.concatenate([mi, mf, jnp.zeros((d, GATE_W - 2 * MLSTM_HEADS), w_in.dtype)], axis=1)
    return jnp.concatenate([mq, mk, mv, mo, gates, aq, ak, av], axis=1).astype(BF16)


def _ab_proj_kernel(x_ref, mod_ref, w_ref, *out_refs, sub):
    h = _modulate(x_ref[...], mod_ref, sub).astype(BF16)
    off = 0
    for (name, width, dtype, scale), o_ref in zip(_AB_SEGS, out_refs):
        z = jnp.dot(h, w_ref[:, off:off + width], preferred_element_type=F32)
        if scale != 1.0:
            z = z * scale
        if name in _AB_SLAB_SEGS:
            for s in range(ATTN_SLABS):
                o_ref[s] = z[:, s * LANES:(s + 1) * LANES]
        else:
            o_ref[...] = z.astype(dtype)
        off += width


def _ab_projection(x2, mod, w_packed, *, sub, seq):
    rows, d = x2.shape
    tiles_per_seq = seq // ROW_TILE
    out_specs, out_shape = [], []
    for name, width, dtype, _ in _AB_SEGS:
        if name in _AB_SLAB_SEGS:
            out_specs.append(pl.BlockSpec((ATTN_SLABS, ROW_TILE, LANES), lambda i: (0, i, 0)))
            out_shape.append(jax.ShapeDtypeStruct((ATTN_SLABS, rows, LANES), dtype))
        else:
            out_specs.append(pl.BlockSpec((ROW_TILE, width), lambda i: (i, 0)))
            out_shape.append(jax.ShapeDtypeStruct((rows, width), dtype))
    return pl.pallas_call(
        functools.partial(_ab_proj_kernel, sub=sub),
        grid=(rows // ROW_TILE,),
        in_specs=[
            pl.BlockSpec((ROW_TILE, d), lambda i: (i, 0)),
            pl.BlockSpec((None, 9, d), lambda i: (i // tiles_per_seq, 0, 0)),
            _const_spec((d, _AB_PACKED_W)),
        ],
        out_specs=out_specs,
        out_shape=out_shape,
        compiler_params=_params(("arbitrary",)),
        name="mixer_ab_in_proj",
    )(x2, mod, w_packed)


def _mlstm_kernel(q_ref, k_ref, v_ref, og_ref, gates_ref, gbias_ref, out_ref, c_ref, n_ref, m_ref):
    L = MLSTM_CHUNK
    H = MLSTM_HEADS
    DK = MLSTM_QK_DIM
    DV = MLSTM_V_DIM

    @pl.when(pl.program_id(1) == 0)
    def _():
        c_ref[...] = jnp.zeros_like(c_ref)
        n_ref[...] = jnp.zeros_like(n_ref)
        m_ref[...] = jnp.zeros_like(m_ref)

    row = lax.broadcasted_iota(jnp.int32, (L, L), 0)
    col = lax.broadcasted_iota(jnp.int32, (L, L), 1)
    causal = row >= col

    gb = gates_ref[...] + gbias_ref[...]
    log_f = -(jnp.maximum(-gb, 0.0) + jnp.log1p(jnp.exp(-jnp.abs(gb))))
    log_f = jnp.where((col >= H) & (col < 2 * H), log_f, 0.0)
    cum_f = jnp.dot(causal.astype(F32), log_f, precision=lax.Precision.HIGHEST,
                    preferred_element_type=F32)
    a = jnp.where(col < H, gb, cum_f)
    a_t = a.T

    w_cols = []
    decays = []
    m_prevs = []
    for h in range(H):
        li_col = a[:, h:h + 1]
        cf_col = a[:, H + h:H + h + 1]
        chunk_f = cf_col[L - 1:L, :]
        m_prev = m_ref[h][0:1, 0:1]
        g_col = chunk_f - cf_col + li_col
        g_max = jnp.max(g_col, axis=0, keepdims=True)
        m_new = jnp.maximum(chunk_f + m_prev, g_max)
        decays.append(jnp.exp(chunk_f + m_prev - m_new))
        w_cols.append(jnp.broadcast_to(jnp.exp(g_col - m_new), (L, DK)))
        m_prevs.append(m_prev)
        m_ref[h] = jnp.broadcast_to(m_new, m_ref.shape[1:])
    kw = k_ref[...].astype(F32) * jnp.concatenate(w_cols, axis=1)
    kw_t = kw.T

    for h in range(H):
        q = q_ref[:, h * DK:(h + 1) * DK]
        k = k_ref[:, h * DK:(h + 1) * DK]
        v = v_ref[:, h * DV:(h + 1) * DV]
        li_row = a_t[h:h + 1, :]
        cf_row = a_t[H + h:H + h + 1, :]
        cf_col = a[:, H + h:H + h + 1]
        m_prev = m_prevs[h]
        c_prev = c_ref[h]
        n_prev = n_ref[h]

        d_log = jnp.where(causal, cf_col - cf_row + li_row, -jnp.inf)
        inter_log = cf_col + m_prev
        m_t = jnp.maximum(inter_log, jnp.max(d_log, axis=1, keepdims=True))
        s = lax.dot_general(q, k, (((1,), (1,)), ((), ())), preferred_element_type=F32)
        w_intra = jnp.exp(d_log - m_t) * s
        w_inter = jnp.exp(inter_log - m_t)
        num = (jnp.dot(w_intra.astype(BF16), v, preferred_element_type=F32)
               + jnp.dot(q, c_prev.astype(BF16), preferred_element_type=F32) * w_inter)
        den = (jnp.sum(w_intra, axis=1, keepdims=True)
               + jnp.sum(q.astype(F32) * n_prev, axis=1, keepdims=True) * w_inter)
        den = jnp.maximum(jnp.abs(den), jnp.exp(-m_t))
        o_gate = jax.nn.sigmoid(og_ref[:, h * DV:(h + 1) * DV])
        out_ref[:, h * DV:(h + 1) * DV] = o_gate * (num / den)

        kw_t_h = kw_t[h * DK:(h + 1) * DK, :]
        c_ref[h] = decays[h] * c_prev + jnp.dot(kw_t_h.astype(BF16), v, preferred_element_type=F32)
        n_ref[h] = decays[h] * n_prev + jnp.sum(kw[:, h * DK:(h + 1) * DK], axis=0, keepdims=True)


def _mlstm(mq, mk, mv, mo, gates, gate_bias, *, batch, seq):
    L = MLSTM_CHUNK
    nc = seq // L

    def spec(w):
        return pl.BlockSpec((L, w), lambda b, c: (b * nc + c, 0))

    return pl.pallas_call(
        _mlstm_kernel,
        grid=(batch, nc),
        in_specs=[spec(MLSTM_QK_W), spec(MLSTM_QK_W), spec(MLSTM_V_W), spec(MLSTM_V_W), spec(GATE_W),
                  pl.BlockSpec((1, GATE_W), lambda b, c: (0, 0))],
        out_specs=spec(MLSTM_V_W),
        out_shape=jax.ShapeDtypeStruct((batch * seq, MLSTM_V_W), F32),
        scratch_shapes=[
            pltpu.VMEM((MLSTM_HEADS, MLSTM_QK_DIM, MLSTM_V_DIM), F32),
            pltpu.VMEM((MLSTM_HEADS, 1, MLSTM_QK_DIM), F32),
            pltpu.VMEM((MLSTM_HEADS, SUBLANES, LANES), F32),
        ],
        compiler_params=_params(("arbitrary", "arbitrary")),
        name="mlstm_chunkwise",
    )(mq, mk, mv, mo, gates, gate_bias)


def _rel_bucket(dist):
    max_exact = REL_BUCKETS // 2
    d = jnp.maximum(dist, 0)
    large = max_exact + (jnp.log(jnp.maximum(d, 1).astype(F32) / max_exact)
                         / math.log(REL_MAX_DIST / max_exact) * (REL_BUCKETS - max_exact)).astype(jnp.int32)
    large = jnp.minimum(large, REL_BUCKETS - 1)
    return jnp.where(d < max_exact, d, large)


def _branch_bias(rel_bias, window, dilation):
    blk = ATTN_BLOCK
    span = window // dilation
    qi = jnp.arange(blk)[:, None]
    kj = jnp.arange(2 * blk)[None, :]
    dist = qi + blk - kj
    valid = (dist >= 0) & (dist <= span)
    one_hot = (_rel_bucket(dist * dilation)[..., None] == jnp.arange(REL_BUCKETS)).astype(F32)
    bias = jnp.einsum('qkb,bh->hqk', one_hot, rel_bias.astype(F32), precision=lax.Precision.HIGHEST)
    return jnp.where(valid[None], bias * LOG2_E, MASK_VALUE)


ATTN_GROUP = ATTN_BLOCK * max(d for _, d in DILATED_BRANCHES)
HEADS_PER_SLAB = LANES // ATTN_HEAD_DIM
FAR_UNROLL = 8
MID_UNROLL = 2
NEAR_UNROLL = 5


def _attn_kernel(q_ref, kp_ref, k_ref, vp_ref, v_ref, bias_ref, out_ref, m_s, l_s, acc_s):
    blk = ATTN_BLOCK
    grp = ATTN_GROUP
    first_group = pl.program_id(2) == 0

    head_a = lax.broadcasted_iota(jnp.int32, (blk, LANES), 1) < ATTN_HEAD_DIM
    no_prev = jnp.where(first_group, MASK_VALUE, 0.0)
    nt = (((1,), (1,)), ((), ()))

    def unit(branch, d, start, *, first, last, prev_in_prev_group):
        def rows(base):
            return pl.ds(base, blk, stride=d) if d > 1 else pl.ds(base, blk)

        q = q_ref[rows(start), :]
        q2 = jnp.concatenate([jnp.where(head_a, q, 0.0), jnp.where(head_a, 0.0, q)], axis=0).astype(BF16)
        if prev_in_prev_group:
            k_prev, v_prev = kp_ref[rows(grp + start - blk * d), :], vp_ref[rows(grp + start - blk * d), :]
        else:
            k_prev, v_prev = k_ref[rows(start - blk * d), :], v_ref[rows(start - blk * d), :]
        kpc = jnp.concatenate([k_prev, k_ref[rows(start), :]], axis=0).astype(BF16)
        vpc = jnp.concatenate([v_prev, v_ref[rows(start), :]], axis=0).astype(BF16)
        s = lax.dot_general(q2, kpc, nt, preferred_element_type=F32) + bias_ref[branch]
        if prev_in_prev_group:
            s = jnp.concatenate([s[:, :blk] + no_prev, s[:, blk:]], axis=1)
        ps, maxes, sums = [], [], []
        for hh in range(HEADS_PER_SLAB):
            sh = s[hh * blk:(hh + 1) * blk, :]
            mx = jnp.max(sh, axis=1, keepdims=True)
            p = jnp.exp2(sh - mx)
            ps.append(p)
            maxes.append(jnp.broadcast_to(mx, (blk, LANES)))
            sums.append(jnp.broadcast_to(jnp.sum(p, axis=1, keepdims=True), (blk, LANES)))
        m_new = jnp.where(head_a, maxes[0], maxes[1])
        l_new = jnp.where(head_a, sums[0], sums[1])
        pv = jnp.dot(jnp.concatenate(ps, axis=0).astype(BF16), vpc, preferred_element_type=F32)
        acc = jnp.where(head_a, pv[:blk], pv[blk:])
        if not first:
            m_old, m_blk = m_s[rows(start), :], m_new
            m_new = jnp.maximum(m_old, m_blk)
            alpha = jnp.exp2(m_old - m_new)
            beta = jnp.exp2(m_blk - m_new)
            l_new = alpha * l_s[rows(start), :] + beta * l_new
            acc = alpha * acc_s[rows(start), :] + beta * acc
        if last:
            out_ref[rows(start), :] = acc / l_new
        else:
            m_s[rows(start), :] = m_new
            l_s[rows(start), :] = l_new
            acc_s[rows(start), :] = acc

    (_, d_near), (_, d_mid), (_, d_far) = DILATED_BRANCHES

    def far_body(i, carry):
        for u in range(FAR_UNROLL):
            unit(2, d_far, i * FAR_UNROLL + u, first=True, last=False, prev_in_prev_group=True)
        return carry

    lax.fori_loop(0, d_far // FAR_UNROLL, far_body, 0)

    blocks_mid = grp // (d_mid * blk)

    def mid_body(i, carry):
        for u in range(MID_UNROLL):
            for j in range(blocks_mid):
                unit(1, d_mid, i * MID_UNROLL + u + j * d_mid * blk, first=False, last=False,
                     prev_in_prev_group=(j == 0))
        return carry

    lax.fori_loop(0, d_mid // MID_UNROLL, mid_body, 0)

    unit(0, d_near, 0, first=False, last=True, prev_in_prev_group=True)

    def near_body(i, carry):
        for u in range(NEAR_UNROLL):
            start = pl.multiple_of((1 + i * NEAR_UNROLL + u) * blk, blk)
            unit(0, d_near, start, first=False, last=True, prev_in_prev_group=False)
        return carry

    lax.fori_loop(0, (grp // blk - 1) // NEAR_UNROLL, near_body, 0)


def _dilated_attention(aq, ak, av, rel_bias, *, batch, seq):
    grp = ATTN_GROUP
    blk = ATTN_BLOCK
    assert [d for _, d in DILATED_BRANCHES] == [1, 4, 16] and (grp // blk - 1) % NEAR_UNROLL == 0
    groups = seq // grp
    rows = batch * seq
    bias = jnp.stack([_branch_bias(rel_bias, w, d).reshape(ATTN_SLABS, HEADS_PER_SLAB * blk, 2 * blk)
                      for w, d in DILATED_BRANCHES])
    slab = pl.BlockSpec((None, grp, LANES), lambda s, b, g: (s, b * groups + g, 0))
    prev = pl.BlockSpec((None, grp, LANES), lambda s, b, g: (s, b * groups + jnp.maximum(g - 1, 0), 0))
    return pl.pallas_call(
        _attn_kernel,
        grid=(ATTN_SLABS, batch, groups),
        in_specs=[slab, prev, slab, prev, slab,
                  pl.BlockSpec((len(DILATED_BRANCHES), None, HEADS_PER_SLAB * blk, 2 * blk),
                               lambda s, b, g: (0, s, 0, 0))],
        out_specs=slab,
        out_shape=jax.ShapeDtypeStruct((ATTN_SLABS, rows, LANES), F32),
        scratch_shapes=[pltpu.VMEM((grp, LANES), F32)] * 3,
        compiler_params=_params(("arbitrary", "arbitrary", "arbitrary")),
        name="dilated_attention",
    )(aq, ak, ak, av, av, bias)


def _ab_out_kernel(x_ref, mod_ref, g_ref, b_ref, ha_ref, hb_ref, w_ref, out_ref, *, sub, alpha):
    hb = jnp.concatenate([hb_ref[s] for s in range(ATTN_SLABS)], axis=1)
    wa = MLSTM_V_W
    y = (jnp.dot(ha_ref[...].astype(BF16), w_ref[:wa, :].astype(BF16), preferred_element_type=F32)
         + jnp.dot(hb.astype(BF16), w_ref[wa:, :].astype(BF16), preferred_element_type=F32))
    out_ref[...] = _post_norm(x_ref[...], y, mod_ref, sub, g_ref, b_ref, alpha, 1.0)


def _ab_output(x2, mod, ln_g, ln_b, ha, hb, w_out, *, sub, alpha, seq):
    rows, d = x2.shape
    tiles_per_seq = seq // ROW_TILE
    row_spec = lambda w: pl.BlockSpec((ROW_TILE, w), lambda i: (i, 0))
    return pl.pallas_call(
        functools.partial(_ab_out_kernel, sub=sub, alpha=alpha),
        grid=(rows // ROW_TILE,),
        in_specs=[row_spec(d),
                  pl.BlockSpec((None, 9, d), lambda i: (i // tiles_per_seq, 0, 0)),
                  _const_spec((1, d)), _const_spec((1, d)),
                  row_spec(MLSTM_V_W),
                  pl.BlockSpec((ATTN_SLABS, ROW_TILE, LANES), lambda i: (0, i, 0)),
                  _const_spec(w_out.shape)],
        out_specs=row_spec(d),
        out_shape=jax.ShapeDtypeStruct((rows, d), F32),
        compiler_params=_params(("arbitrary",)),
        name="mixer_ab_out_proj",
    )(x2, mod, ln_g, ln_b, ha, hb, w_out)


def _gelu_tanh(x):
    return 0.5 * x * (1.0 + jnp.tanh(math.sqrt(2.0 / math.pi) * (x + 0.044715 * (x * x * x))))


def _cd_kernel(x_ref, mod_ref, g_ref, b_ref, w_in_ref, w_out_ref, conv_w_ref, conv_b_ref,
               sgu_g_ref, sgu_b_ref, sgu_w_ref, sgu_bias_ref, out_ref, conv_ref, *, sub, alpha, tiles_per_seq):
    tm = x_ref.shape[0]
    halo = SUBLANES

    @pl.when(pl.program_id(0) % tiles_per_seq == 0)
    def _():
        conv_ref[0:halo, :] = jnp.zeros((halo, CONV_CH), F32)

    x = x_ref[...]
    h = _modulate(x, mod_ref, sub).astype(BF16)

    def proj(idx, width):
        return jnp.dot(h, w_in_ref[:, idx:idx + width].astype(BF16), preferred_element_type=F32)

    gate_b = proj(0, CONV_CH)
    gate_c = proj(CONV_CH, CONV_CH)
    xc = proj(2 * CONV_CH, CONV_CH)
    u = proj(3 * CONV_CH, SGU_CH)
    v = proj(3 * CONV_CH + SGU_CH, SGU_CH)

    conv_ref[halo:halo + tm, :] = gate_c * xc
    conv = conv_b_ref[...]
    for j in range(CONV_K):
        start = halo - (CONV_K - 1) + j
        conv = conv + conv_w_ref[j:j + 1, :] * conv_ref[start:start + tm, :]
    conv_ref[0:halo, :] = conv_ref[tm:tm + halo, :]
    y_c = gate_b * conv

    u = _gelu_tanh(u)
    v = _gelu_tanh(v)
    row = lax.broadcasted_iota(jnp.int32, (SGU_CHUNK, SGU_CHUNK), 0)
    col = lax.broadcasted_iota(jnp.int32, (SGU_CHUNK, SGU_CHUNK), 1)
    y_d_groups = []
    for g in range(SGU_GROUPS):
        gs = slice(g * SGU_GROUP_CH, (g + 1) * SGU_GROUP_CH)
        vg = v[:, gs]
        mu = jnp.mean(vg, axis=-1, keepdims=True)
        vc = vg - mu
        var = jnp.mean(vc * vc, axis=-1, keepdims=True)
        vn = (vc * lax.rsqrt(var + LN_EPS) * sgu_g_ref[:, gs] + sgu_b_ref[:, gs]).astype(BF16)
        w_s = jnp.where(row >= col, sgu_w_ref[g], 0.0).astype(BF16)
        mixed = [jnp.dot(w_s, vn[t0:t0 + SGU_CHUNK, :], preferred_element_type=F32) + sgu_bias_ref[g]
                 for t0 in range(0, tm, SGU_CHUNK)]
        y_d_groups.append(u[:, gs] * jnp.concatenate(mixed, axis=0))
    y_d = jnp.concatenate(y_d_groups, axis=1)

    y = (jnp.dot(y_c.astype(BF16), w_out_ref[:CONV_CH, :].astype(BF16), preferred_element_type=F32)
         + jnp.dot(y_d.astype(BF16), w_out_ref[CONV_CH:, :].astype(BF16), preferred_element_type=F32))
    out_ref[...] = _post_norm(x, y, mod_ref, sub, g_ref, b_ref, alpha, 1.0)


def _cd_sublayer(x2, mod, ln_g, ln_b, w_in, w_out, conv_w, conv_b, sgu_g, sgu_b, sgu_w, sgu_bias,
                 *, sub, alpha, seq):
    rows, d = x2.shape
    tiles_per_seq = seq // ROW_TILE
    return pl.pallas_call(
        functools.partial(_cd_kernel, sub=sub, alpha=alpha, tiles_per_seq=tiles_per_seq),
        grid=(rows // ROW_TILE,),
        in_specs=[
            pl.BlockSpec((ROW_TILE, d), lambda i: (i, 0)),
            pl.BlockSpec((None, 9, d), lambda i: (i // tiles_per_seq, 0, 0)),
            _const_spec((1, d)), _const_spec((1, d)),
            _const_spec(w_in.shape), _const_spec(w_out.shape),
            _const_spec(conv_w.shape), _const_spec(conv_b.shape),
            _const_spec(sgu_g.shape), _const_spec(sgu_b.shape),
            _const_spec(sgu_w.shape), _const_spec(sgu_bias.shape),
        ],
        out_specs=pl.BlockSpec((ROW_TILE, d), lambda i: (i, 0)),
        out_shape=jax.ShapeDtypeStruct((rows, d), F32),
        scratch_shapes=[pltpu.VMEM((ROW_TILE + 2 * SUBLANES, CONV_CH), F32)],
        compiler_params=_params(("arbitrary",)),
        name="mixer_cd_sublayer",
    )(x2, mod, ln_g, ln_b, w_in, w_out, conv_w, conv_b, sgu_g, sgu_b, sgu_w, sgu_bias)


def kernel(x, c, rel_bias, ada_w, ada_b, ln_g, ln_b, ffn_w_gate, ffn_w_up, ffn_w_down, ab_w_in, ab_w_out,
           ab_b_igate, ab_b_fgate, cd_w_in, cd_w_out, cd_conv_w, cd_conv_b, cd_sgu_ln_g, cd_sgu_ln_b,
           cd_sgu_w, cd_sgu_b):
    batch, seq, d = x.shape
    depth = ada_w.shape[0]
    alpha = (2 * depth) ** 0.25
    assert seq % ROW_TILE == 0 and ROW_TILE % SGU_CHUNK == 0
    assert all(seq % (dil * ATTN_BLOCK) == 0 and win // dil == ATTN_BLOCK for win, dil in DILATED_BRANCHES)

    mods = _modulation(c, ada_w, ada_b)
    x2 = x.reshape(batch * seq, d)
    for layer in range(depth):
        mod = mods[layer]
        lg = ln_g[layer][:, None, :]
        lb = ln_b[layer][:, None, :]
        ffn = functools.partial(_ffn_sublayer, wg=ffn_w_gate, wu=ffn_w_up, wd=ffn_w_down, layer=layer,
                                alpha=alpha, seq=seq)
        x2 = ffn(x2, mod, lg[0], lb[0], which=0, sub=0)
        if layer % 2 == 0:
            e = layer // 2
            mq, mk, mv, mo, gates, aq, ak, av = _ab_projection(x2, mod, _pack_ab_w_in(ab_w_in[e]), sub=1, seq=seq)
            gate_bias = jnp.concatenate(
                [ab_b_igate[e], ab_b_fgate[e], jnp.zeros((GATE_W - 2 * MLSTM_HEADS,), F32)])[None, :]
            ha = _mlstm(mq, mk, mv, mo, gates, gate_bias, batch=batch, seq=seq)
            hb = _dilated_attention(aq, ak, av, rel_bias, batch=batch, seq=seq)
            x2 = _ab_output(x2, mod, lg[1], lb[1], ha, hb, ab_w_out[e],
                            sub=1, alpha=alpha, seq=seq)
        else:
            o = layer // 2
            sgu_bias = jnp.broadcast_to(cd_sgu_b[o][:, :, None], (SGU_GROUPS, SGU_CHUNK, SGU_GROUP_CH))
            x2 = _cd_sublayer(x2, mod, lg[1], lb[1], cd_w_in[o], cd_w_out[o],
                              cd_conv_w[o], cd_conv_b[o][None, :], cd_sgu_ln_g[o].reshape(1, SGU_CH),
                              cd_sgu_ln_b[o].reshape(1, SGU_CH), cd_sgu_w[o], sgu_bias,
                              sub=1, alpha=alpha, seq=seq)
        x2 = ffn(x2, mod, lg[2], lb[2], which=1, sub=2)
    return x2.reshape(batch, seq, d)
```

```python
import functools
import math

import jax
import jax.numpy as jnp
import numpy as np
from jax import lax
from jax.experimental import pallas as pl
from jax.experimental.pallas import tpu as pltpu

F32 = jnp.float32
BF16 = jnp.bfloat16

MLSTM_HEADS = 4
MLSTM_QK_DIM = 64
MLSTM_V_DIM = 128
MLSTM_CHUNK = 128
ATTN_HEADS = 8
ATTN_HEAD_DIM = 64
DILATED_BRANCHES = ((128, 1), (512, 4), (2048, 16))
ATTN_BLOCK = 128
REL_BUCKETS = 32
REL_MAX_DIST = 2048
CONV_CH = 512
CONV_K = 3
SGU_CH = 512
SGU_GROUPS = 4
SGU_GROUP_CH = SGU_CH // SGU_GROUPS
SGU_CHUNK = 128
FFN_RES_W = 0.5
LN_EPS = 1e-5
MASK_VALUE = -1e30
LOG2_E = math.log2(math.e)

MLSTM_QK_W = MLSTM_HEADS * MLSTM_QK_DIM
MLSTM_V_W = MLSTM_HEADS * MLSTM_V_DIM
ATTN_W = ATTN_HEADS * ATTN_HEAD_DIM
GATE_W = 128

LANES = 128
SUBLANES = 8
VMEM_LIMIT_BYTES = 56 * 1024 * 1024

ROW_TILE = 512
FF_CHUNK = 256


def _const_spec(shape):
    nd = len(shape)
    return pl.BlockSpec(shape, lambda *_: (0,) * nd, pipeline_mode=pl.Buffered(1))


def _params(sem):
    return pltpu.CompilerParams(dimension_semantics=sem, vmem_limit_bytes=VMEM_LIMIT_BYTES)


def _modulate(x, mod_ref, sub):
    shift = mod_ref[3 * sub:3 * sub + 1, :]
    scale = mod_ref[3 * sub + 1:3 * sub + 2, :]
    return x * (1.0 + scale) + shift


def _post_norm(x, y, mod_ref, sub, g_ref, b_ref, alpha, res_w):
    gate = mod_ref[3 * sub + 2:3 * sub + 3, :]
    r = alpha * x + res_w * (1.0 + gate) * y
    mu = jnp.mean(r, axis=-1, keepdims=True)
    rc = r - mu
    var = jnp.mean(rc * rc, axis=-1, keepdims=True)
    return rc * lax.rsqrt(var + LN_EPS) * g_ref[...] + b_ref[...]


def _mod_kernel(c_ref, w_ref, b_ref, o_ref):
    c = c_ref[...]
    sc = c * jax.nn.sigmoid(c)
    o_ref[...] = jnp.dot(sc, w_ref[...], preferred_element_type=F32) + b_ref[...]


def _modulation(c, ada_w, ada_b):
    depth, d, n = ada_w.shape
    b = c.shape[0]
    bp = -(-b // SUBLANES) * SUBLANES
    cp = jnp.pad(c, ((0, bp - b), (0, 0)))
    bn = 1024
    out = pl.pallas_call(
        _mod_kernel,
        grid=(depth, n // bn),
        in_specs=[
            pl.BlockSpec((bp, d), lambda l, j: (0, 0)),
            pl.BlockSpec((None, d, bn), lambda l, j: (l, 0, j)),
            pl.BlockSpec((None, 1, bn), lambda l, j: (l, 0, j)),
        ],
        out_specs=pl.BlockSpec((None, bp, bn), lambda l, j: (l, 0, j)),
        out_shape=jax.ShapeDtypeStruct((depth, bp, n), F32),
        compiler_params=_params(("arbitrary", "arbitrary")),
        name="adaln_mod",
    )(cp, ada_w, ada_b.reshape(depth, 1, n))
    return out[:, :b].reshape(depth, b, 9, d)


def _ffn_kernel(x_ref, mod_ref, g_ref, b_ref, wg_ref, wu_ref, wd_ref, o_ref, *, sub, alpha):
    x = x_ref[...]
    h = _modulate(x, mod_ref, sub).astype(BF16)
    d_ff = wg_ref.shape[1]
    acc = jnp.zeros(x.shape, F32)
    for c0 in range(0, d_ff, FF_CHUNK):
        g = jnp.dot(h, wg_ref[:, c0:c0 + FF_CHUNK].astype(BF16), preferred_element_type=F32)
        u = jnp.dot(h, wu_ref[:, c0:c0 + FF_CHUNK].astype(BF16), preferred_element_type=F32)
        a = (g * jax.nn.sigmoid(g) * u).astype(BF16)
        acc = acc + jnp.dot(a, wd_ref[c0:c0 + FF_CHUNK, :].astype(BF16), preferred_element_type=F32)
    o_ref[...] = _post_norm(x, acc, mod_ref, sub, g_ref, b_ref, alpha, FFN_RES_W)


def _ffn_sublayer(x2, mod, ln_g, ln_b, wg, wu, wd, *, layer, which, sub, alpha, seq):
    rows, d = x2.shape
    d_ff = wg.shape[-1]
    tiles_per_seq = seq // ROW_TILE

    def weight_spec(shape):
        return pl.BlockSpec((None, None) + shape, lambda i: (layer, which, 0, 0), pipeline_mode=pl.Buffered(1))

    return pl.pallas_call(
        functools.partial(_ffn_kernel, sub=sub, alpha=alpha),
        grid=(rows // ROW_TILE,),
        in_specs=[
            pl.BlockSpec((ROW_TILE, d), lambda i: (i, 0)),
            pl.BlockSpec((None, 9, d), lambda i: (i // tiles_per_seq, 0, 0)),
            _const_spec((1, d)),
            _const_spec((1, d)),
            weight_spec((d, d_ff)),
            weight_spec((d, d_ff)),
            weight_spec((d_ff, d)),
        ],
        out_specs=pl.BlockSpec((ROW_TILE, d), lambda i: (i, 0)),
        out_shape=jax.ShapeDtypeStruct((rows, d), F32),
        compiler_params=_params(("arbitrary",)),
        name="swiglu_sublayer",
    )(x2, mod, ln_g, ln_b, wg, wu, wd)


_AB_SEGS = (
    ("mq", MLSTM_QK_W, BF16, 1.0),
    ("mk", MLSTM_QK_W, BF16, MLSTM_QK_DIM ** -0.5),
    ("mv", MLSTM_V_W, BF16, 1.0),
    ("mo", MLSTM_V_W, F32, 1.0),
    ("gates", GATE_W, F32, 1.0),
    ("aq", ATTN_W, F32, ATTN_HEAD_DIM ** -0.5 * LOG2_E),
    ("ak", ATTN_W, F32, 1.0),
    ("av", ATTN_W, F32, 1.0),
)
_AB_PACKED_W = sum(s[1] for s in _AB_SEGS)
_AB_SLAB_SEGS = ("aq", "ak", "av")
ATTN_SLABS = ATTN_W // LANES


def _pack_ab_w_in(w_in):
    d = w_in.shape[0]
    sizes = (MLSTM_QK_W, MLSTM_QK_W, MLSTM_V_W, MLSTM_V_W, MLSTM_HEADS, MLSTM_HEADS, ATTN_W, ATTN_W, ATTN_W)
    mq, mk, mv, mo, mi, mf, aq, ak, av = jnp.split(w_in, np.cumsum(sizes)[:-1].tolist(), axis=1)
    gates = jnp.concatenate([mi, mf, jnp.zeros((d, GATE_W - 2 * MLSTM_HEADS), w_in.dtype)], axis=1)
    return jnp.concatenate([mq, mk, mv, mo, gates, aq, ak, av], axis=1).astype(BF16)


def _ab_proj_kernel(x_ref, mod_ref, w_ref, gbias_ref, mq_ref, mk_ref, mkt_ref, mv_ref, mo_ref, grow_ref, cfb_ref,
                    aq_ref, ak_ref, av_ref, *, sub):
    L = MLSTM_CHUNK
    H = MLSTM_HEADS
    tm = x_ref.shape[0]
    h = _modulate(x_ref[...], mod_ref, sub).astype(BF16)

    def proj(name):
        off = 0
        for seg, width, _, scale in _AB_SEGS:
            if seg == name:
                z = jnp.dot(h, w_ref[:, off:off + width], preferred_element_type=F32)
                return z if scale == 1.0 else z * scale
            off += width
        raise KeyError(name)

    gb = proj("gates") + gbias_ref[...]
    log_f = -(jnp.maximum(-gb, 0.0) + jnp.log1p(jnp.exp(-jnp.abs(gb))))
    row = lax.broadcasted_iota(jnp.int32, (L, L), 0)
    col = lax.broadcasted_iota(jnp.int32, (L, L), 1)
    tril = (row >= col).astype(F32)
    is_f = (col >= H) & (col < 2 * H)

    zk = proj("mk")
    mk_ref[...] = zk.astype(BF16)
    mq_ref[...] = proj("mq").astype(BF16)

    gate_cols = []
    for c in range(tm // L):
        ch = slice(c * L, (c + 1) * L)
        cum_f = jnp.dot(tril, jnp.where(is_f, log_f[ch], 0.0), precision=lax.Precision.HIGHEST,
                        preferred_element_type=F32)
        gate_cols.append(jnp.where(col < H, gb[ch], cum_f))

    mv_ref[...] = proj("mv").astype(BF16)
    kw = MLSTM_QK_W
    for c in range(tm // L):
        mkt_ref[c * kw:(c + 1) * kw, :] = zk[c * L:(c + 1) * L, :].T.astype(BF16)
    mo_ref[...] = proj("mo")

    for c, a in enumerate(gate_cols):
        grow_ref[c * SUBLANES:(c + 1) * SUBLANES, :] = a.T[0:SUBLANES, :]
        for hd in range(H):
            cfb_ref[c * L:(c + 1) * L, hd * LANES:(hd + 1) * LANES] = jnp.broadcast_to(
                a[:, H + hd:H + hd + 1], (L, LANES))

    for name, o_ref in (("aq", aq_ref), ("ak", ak_ref), ("av", av_ref)):
        z = proj(name)
        for s in range(ATTN_SLABS):
            o_ref[s] = z[:, s * LANES:(s + 1) * LANES]

def _ab_projection(x2, mod, w_packed, gate_bias, *, sub, seq):
    rows, d = x2.shape
    tiles_per_seq = seq // ROW_TILE
    chunks = ROW_TILE // MLSTM_CHUNK
    row_block = lambda r, w: pl.BlockSpec((r, w), lambda i: (i, 0))
    slab_block = pl.BlockSpec((ATTN_SLABS, ROW_TILE, LANES), lambda i: (0, i, 0))
    slab_shape = jax.ShapeDtypeStruct((ATTN_SLABS, rows, LANES), F32)
    outs = [
        (row_block(ROW_TILE, MLSTM_QK_W), jax.ShapeDtypeStruct((rows, MLSTM_QK_W), BF16)),
        (row_block(ROW_TILE, MLSTM_QK_W), jax.ShapeDtypeStruct((rows, MLSTM_QK_W), BF16)),
        (row_block(chunks * MLSTM_QK_W, MLSTM_CHUNK),
         jax.ShapeDtypeStruct((rows // MLSTM_CHUNK * MLSTM_QK_W, MLSTM_CHUNK), BF16)),
        (row_block(ROW_TILE, MLSTM_V_W), jax.ShapeDtypeStruct((rows, MLSTM_V_W), BF16)),
        (row_block(ROW_TILE, MLSTM_V_W), jax.ShapeDtypeStruct((rows, MLSTM_V_W), F32)),
        (row_block(chunks * SUBLANES, MLSTM_CHUNK),
         jax.ShapeDtypeStruct((rows // MLSTM_CHUNK * SUBLANES, MLSTM_CHUNK), F32)),
        (row_block(ROW_TILE, MLSTM_HEADS * LANES), jax.ShapeDtypeStruct((rows, MLSTM_HEADS * LANES), F32)),
        (slab_block, slab_shape), (slab_block, slab_shape), (slab_block, slab_shape),
    ]
    return pl.pallas_call(
        functools.partial(_ab_proj_kernel, sub=sub),
        grid=(rows // ROW_TILE,),
        in_specs=[
            pl.BlockSpec((ROW_TILE, d), lambda i: (i, 0)),
            pl.BlockSpec((None, 9, d), lambda i: (i // tiles_per_seq, 0, 0)),
            _const_spec((d, _AB_PACKED_W)),
            _const_spec((1, GATE_W)),
        ],
        out_specs=[o[0] for o in outs],
        out_shape=[o[1] for o in outs],
        compiler_params=_params(("arbitrary",)),
        name="mixer_ab_in_proj",
    )(x2, mod, w_packed, gate_bias)


def _mlstm_kernel(q_ref, k_ref, kt_ref, v_ref, og_ref, grow_ref, cfb_ref, out_ref, c_ref, m_ref):
    L = MLSTM_CHUNK
    H = MLSTM_HEADS
    DK = MLSTM_QK_DIM
    DV = MLSTM_V_DIM

    @pl.when(pl.program_id(1) == 0)
    def _():
        c_ref[...] = jnp.zeros_like(c_ref)
        m_ref[...] = jnp.zeros_like(m_ref)

    causal = lax.broadcasted_iota(jnp.int32, (L, L), 0) >= lax.broadcasted_iota(jnp.int32, (L, L), 1)
    ones = jnp.ones((L, LANES), BF16)

    for h in range(H):
        q = q_ref[:, h * DK:(h + 1) * DK]
        k = k_ref[:, h * DK:(h + 1) * DK]
        v1 = jnp.concatenate([v_ref[:, h * DV:(h + 1) * DV], ones], axis=1)
        li_row = grow_ref[h:h + 1, :]
        cf_row = grow_ref[H + h:H + h + 1, :]
        cf = cfb_ref[:, h * LANES:(h + 1) * LANES]
        m_prev = m_ref[h][0:1, 0:1]
        cn_prev = c_ref[h]

        d_log = jnp.where(causal, cf - cf_row + li_row, -jnp.inf)
        inter_log = cf + m_prev
        m_t = jnp.maximum(inter_log, jnp.broadcast_to(jnp.max(d_log, axis=1, keepdims=True), (L, L)))
        s = lax.dot_general(q, k, (((1,), (1,)), ((), ())), preferred_element_type=F32)
        w_intra = (jnp.exp(d_log - m_t) * s).astype(BF16)
        w_inter = jnp.exp(inter_log - m_t)
        intra = jnp.dot(w_intra, v1, preferred_element_type=F32)
        inter = jnp.dot(q, cn_prev.astype(BF16), preferred_element_type=F32)
        num = intra[:, :DV] + inter[:, :DV] * w_inter
        den = intra[:, DV:] + inter[:, DV:] * w_inter
        den = jnp.maximum(jnp.abs(den), jnp.exp(-m_t))
        o_gate = jax.nn.sigmoid(og_ref[:, h * DV:(h + 1) * DV])
        out_ref[:, h * DV:(h + 1) * DV] = o_gate * (num / den)

        chunk_f = cf_row[:, L - 1:L]
        g_row = chunk_f - cf_row + li_row
        m_new = jnp.maximum(chunk_f + m_prev, jnp.max(g_row, axis=1, keepdims=True))
        decay = jnp.exp(chunk_f + m_prev - m_new)
        kw_t = (kt_ref[h * DK:(h + 1) * DK, :].astype(F32) * jnp.exp(g_row - m_new)).astype(BF16)
        c_ref[h] = decay * cn_prev + jnp.dot(kw_t, v1, preferred_element_type=F32)
        m_ref[h] = jnp.broadcast_to(m_new, m_ref.shape[1:])


def _mlstm(mq, mk, mkt, mv, mo, grow, cfb, *, batch, seq):
    L = MLSTM_CHUNK
    nc = seq // L
    assert MLSTM_V_DIM == LANES and MLSTM_CHUNK == LANES and 2 * MLSTM_HEADS <= SUBLANES

    def spec(r, w):
        return pl.BlockSpec((r, w), lambda b, c: (b * nc + c, 0))

    return pl.pallas_call(
        _mlstm_kernel,
        grid=(batch, nc),
        in_specs=[spec(L, MLSTM_QK_W), spec(L, MLSTM_QK_W), spec(MLSTM_QK_W, L), spec(L, MLSTM_V_W),
                  spec(L, MLSTM_V_W), spec(SUBLANES, L), spec(L, MLSTM_HEADS * LANES)],
        out_specs=spec(L, MLSTM_V_W),
        out_shape=jax.ShapeDtypeStruct((batch * seq, MLSTM_V_W), F32),
        scratch_shapes=[
            pltpu.VMEM((MLSTM_HEADS, MLSTM_QK_DIM, MLSTM_V_DIM + LANES), F32),
            pltpu.VMEM((MLSTM_HEADS, SUBLANES, LANES), F32),
        ],
        compiler_params=_params(("arbitrary", "arbitrary")),
        name="mlstm_chunkwise",
    )(mq, mk, mkt, mv, mo, grow, cfb)


def _rel_bucket(dist):
    max_exact = REL_BUCKETS // 2
    d = jnp.maximum(dist, 0)
    large = max_exact + (jnp.log(jnp.maximum(d, 1).astype(F32) / max_exact)
                         / math.log(REL_MAX_DIST / max_exact) * (REL_BUCKETS - max_exact)).astype(jnp.int32)
    large = jnp.minimum(large, REL_BUCKETS - 1)
    return jnp.where(d < max_exact, d, large)


def _branch_bias(rel_bias, window, dilation):
    blk = ATTN_BLOCK
    span = window // dilation
    qi = jnp.arange(blk)[:, None]
    kj = jnp.arange(2 * blk)[None, :]
    dist = qi + blk - kj
    valid = (dist >= 0) & (dist <= span)
    one_hot = (_rel_bucket(dist * dilation)[..., None] == jnp.arange(REL_BUCKETS)).astype(F32)
    bias = jnp.einsum('qkb,bh->hqk', one_hot, rel_bias.astype(F32), precision=lax.Precision.HIGHEST)
    return jnp.where(valid[None], bias * LOG2_E, MASK_VALUE)


ATTN_GROUP = ATTN_BLOCK * max(d for _, d in DILATED_BRANCHES)
HEADS_PER_SLAB = LANES // ATTN_HEAD_DIM
FAR_UNROLL = 8
MID_UNROLL = 2
NEAR_UNROLL = 5


def _attn_kernel(q_ref, kp_ref, k_ref, vp_ref, v_ref, bias_ref, out_ref, m_s, l_s, acc_s):
    blk = ATTN_BLOCK
    grp = ATTN_GROUP
    first_group = pl.program_id(2) == 0

    head_a = lax.broadcasted_iota(jnp.int32, (blk, LANES), 1) < ATTN_HEAD_DIM
    no_prev = jnp.where(first_group, MASK_VALUE, 0.0)
    nt = (((1,), (1,)), ((), ()))

    def unit(branch, d, start, *, first, last, prev_in_prev_group):
        def rows(base):
            return pl.ds(base, blk, stride=d) if d > 1 else pl.ds(base, blk)

        q = q_ref[rows(start), :]
        q2 = jnp.concatenate([jnp.where(head_a, q, 0.0), jnp.where(head_a, 0.0, q)], axis=0).astype(BF16)
        if prev_in_prev_group:
            k_prev, v_prev = kp_ref[rows(grp + start - blk * d), :], vp_ref[rows(grp + start - blk * d), :]
        else:
            k_prev, v_prev = k_ref[rows(start - blk * d), :], v_ref[rows(start - blk * d), :]
        kpc = jnp.concatenate([k_prev, k_ref[rows(start), :]], axis=0).astype(BF16)
        vpc = jnp.concatenate([v_prev, v_ref[rows(start), :]], axis=0).astype(BF16)
        s = lax.dot_general(q2, kpc, nt, preferred_element_type=F32) + bias_ref[branch]
        if prev_in_prev_group:
            s = jnp.concatenate([s[:, :blk] + no_prev, s[:, blk:]], axis=1)
        ps, maxes, sums = [], [], []
        for hh in range(HEADS_PER_SLAB):
            sh = s[hh * blk:(hh + 1) * blk, :]
            mx = jnp.max(sh, axis=1, keepdims=True)
            p = jnp.exp2(sh - mx)
            ps.append(p)
            maxes.append(jnp.broadcast_to(mx, (blk, LANES)))
            sums.append(jnp.broadcast_to(jnp.sum(p, axis=1, keepdims=True), (blk, LANES)))
        m_new = jnp.where(head_a, maxes[0], maxes[1])
        l_new = jnp.where(head_a, sums[0], sums[1])
        pv = jnp.dot(jnp.concatenate(ps, axis=0).astype(BF16), vpc, preferred_element_type=F32)
        acc = jnp.where(head_a, pv[:blk], pv[blk:])
        if not first:
            m_old, m_blk = m_s[rows(start), :], m_new
            m_new = jnp.maximum(m_old, m_blk)
            alpha = jnp.exp2(m_old - m_new)
            beta = jnp.exp2(m_blk - m_new)
            l_new = alpha * l_s[rows(start), :] + beta * l_new
            acc = alpha * acc_s[rows(start), :] + beta * acc
        if last:
            out_ref[rows(start), :] = acc / l_new
        else:
            m_s[rows(start), :] = m_new
            l_s[rows(start), :] = l_new
            acc_s[rows(start), :] = acc

    (_, d_near), (_, d_mid), (_, d_far) = DILATED_BRANCHES

    def far_body(i, carry):
        for u in range(FAR_UNROLL):
            unit(2, d_far, i * FAR_UNROLL + u, first=True, last=False, prev_in_prev_group=True)
        return carry

    lax.fori_loop(0, d_far // FAR_UNROLL, far_body, 0)

    blocks_mid = grp // (d_mid * blk)

    def mid_body(i, carry):
        for u in range(MID_UNROLL):
            for j in range(blocks_mid):
                unit(1, d_mid, i * MID_UNROLL + u + j * d_mid * blk, first=False, last=False,
                     prev_in_prev_group=(j == 0))
        return carry

    lax.fori_loop(0, d_mid // MID_UNROLL, mid_body, 0)

    unit(0, d_near, 0, first=False, last=True, prev_in_prev_group=True)

    def near_body(i, carry):
        for u in range(NEAR_UNROLL):
            start = pl.multiple_of((1 + i * NEAR_UNROLL + u) * blk, blk)
            unit(0, d_near, start, first=False, last=True, prev_in_prev_group=False)
        return carry

    lax.fori_loop(0, (grp // blk - 1) // NEAR_UNROLL, near_body, 0)


def _dilated_attention(aq, ak, av, rel_bias, *, batch, seq):
    grp = ATTN_GROUP
    blk = ATTN_BLOCK
    assert [d for _, d in DILATED_BRANCHES] == [1, 4, 16] and (grp // blk - 1) % NEAR_UNROLL == 0
    groups = seq // grp
    rows = batch * seq
    bias = jnp.stack([_branch_bias(rel_bias, w, d).reshape(ATTN_SLABS, HEADS_PER_SLAB * blk, 2 * blk)
                      for w, d in DILATED_BRANCHES])
    slab = pl.BlockSpec((None, grp, LANES), lambda s, b, g: (s, b * groups + g, 0))
    prev = pl.BlockSpec((None, grp, LANES), lambda s, b, g: (s, b * groups + jnp.maximum(g - 1, 0), 0))
    return pl.pallas_call(
        _attn_kernel,
        grid=(ATTN_SLABS, batch, groups),
        in_specs=[slab, prev, slab, prev, slab,
                  pl.BlockSpec((len(DILATED_BRANCHES), None, HEADS_PER_SLAB * blk, 2 * blk),
                               lambda s, b, g: (0, s, 0, 0))],
        out_specs=slab,
        out_shape=jax.ShapeDtypeStruct((ATTN_SLABS, rows, LANES), F32),
        scratch_shapes=[pltpu.VMEM((grp, LANES), F32)] * 3,
        compiler_params=_params(("arbitrary", "arbitrary", "arbitrary")),
        name="dilated_attention",
    )(aq, ak, ak, av, av, bias)


def _ab_out_kernel(x_ref, mod_ref, g_ref, b_ref, ha_ref, hb_ref, w_ref, out_ref, *, sub, alpha):
    hb = jnp.concatenate([hb_ref[s] for s in range(ATTN_SLABS)], axis=1)
    wa = MLSTM_V_W
    y = (jnp.dot(ha_ref[...].astype(BF16), w_ref[:wa, :].astype(BF16), preferred_element_type=F32)
         + jnp.dot(hb.astype(BF16), w_ref[wa:, :].astype(BF16), preferred_element_type=F32))
    out_ref[...] = _post_norm(x_ref[...], y, mod_ref, sub, g_ref, b_ref, alpha, 1.0)


def _ab_output(x2, mod, ln_g, ln_b, ha, hb, w_out, *, sub, alpha, seq):
    rows, d = x2.shape
    tiles_per_seq = seq // ROW_TILE
    row_spec = lambda w: pl.BlockSpec((ROW_TILE, w), lambda i: (i, 0))
    return pl.pallas_call(
        functools.partial(_ab_out_kernel, sub=sub, alpha=alpha),
        grid=(rows // ROW_TILE,),
        in_specs=[row_spec(d),
                  pl.BlockSpec((None, 9, d), lambda i: (i // tiles_per_seq, 0, 0)),
                  _const_spec((1, d)), _const_spec((1, d)),
                  row_spec(MLSTM_V_W),
                  pl.BlockSpec((ATTN_SLABS, ROW_TILE, LANES), lambda i: (0, i, 0)),
                  _const_spec(w_out.shape)],
        out_specs=row_spec(d),
        out_shape=jax.ShapeDtypeStruct((rows, d), F32),
        compiler_params=_params(("arbitrary",)),
        name="mixer_ab_out_proj",
    )(x2, mod, ln_g, ln_b, ha, hb, w_out)


def _gelu_tanh(x):
    return 0.5 * x * (1.0 + jnp.tanh(math.sqrt(2.0 / math.pi) * (x + 0.044715 * (x * x * x))))


def _cd_kernel(x_ref, mod_ref, g_ref, b_ref, w_in_ref, w_out_ref, conv_w_ref, conv_b_ref,
               sgu_g_ref, sgu_b_ref, sgu_w_ref, sgu_bias_ref, out_ref, conv_ref, *, sub, alpha, tiles_per_seq):
    tm = x_ref.shape[0]
    halo = SUBLANES

    @pl.when(pl.program_id(0) % tiles_per_seq == 0)
    def _():
        conv_ref[0:halo, :] = jnp.zeros((halo, CONV_CH), F32)

    x = x_ref[...]
    h = _modulate(x, mod_ref, sub).astype(BF16)

    def proj(idx, width):
        return jnp.dot(h, w_in_ref[:, idx:idx + width].astype(BF16), preferred_element_type=F32)

    gate_b = proj(0, CONV_CH)
    gate_c = proj(CONV_CH, CONV_CH)
    xc = proj(2 * CONV_CH, CONV_CH)
    u = proj(3 * CONV_CH, SGU_CH)
    v = proj(3 * CONV_CH + SGU_CH, SGU_CH)

    conv_ref[halo:halo + tm, :] = gate_c * xc
    conv = conv_b_ref[...]
    for j in range(CONV_K):
        start = halo - (CONV_K - 1) + j
        conv = conv + conv_w_ref[j:j + 1, :] * conv_ref[start:start + tm, :]
    conv_ref[0:halo, :] = conv_ref[tm:tm + halo, :]
    y_c = gate_b * conv

    u = _gelu_tanh(u)
    v = _gelu_tanh(v)
    row = lax.broadcasted_iota(jnp.int32, (SGU_CHUNK, SGU_CHUNK), 0)
    col = lax.broadcasted_iota(jnp.int32, (SGU_CHUNK, SGU_CHUNK), 1)
    y_d_groups = []
    for g in range(SGU_GROUPS):
        gs = slice(g * SGU_GROUP_CH, (g + 1) * SGU_GROUP_CH)
        vg = v[:, gs]
        mu = jnp.mean(vg, axis=-1, keepdims=True)
        vc = vg - mu
        var = jnp.mean(vc * vc, axis=-1, keepdims=True)
        vn = (vc * lax.rsqrt(var + LN_EPS) * sgu_g_ref[:, gs] + sgu_b_ref[:, gs]).astype(BF16)
        w_s = jnp.where(row >= col, sgu_w_ref[g], 0.0).astype(BF16)
        mixed = [jnp.dot(w_s, vn[t0:t0 + SGU_CHUNK, :], preferred_element_type=F32) + sgu_bias_ref[g]
                 for t0 in range(0, tm, SGU_CHUNK)]
        y_d_groups.append(u[:, gs] * jnp.concatenate(mixed, axis=0))
    y_d = jnp.concatenate(y_d_groups, axis=1)

    y = (jnp.dot(y_c.astype(BF16), w_out_ref[:CONV_CH, :].astype(BF16), preferred_element_type=F32)
         + jnp.dot(y_d.astype(BF16), w_out_ref[CONV_CH:, :].astype(BF16), preferred_element_type=F32))
    out_ref[...] = _post_norm(x, y, mod_ref, sub, g_ref, b_ref, alpha, 1.0)


def _cd_sublayer(x2, mod, ln_g, ln_b, w_in, w_out, conv_w, conv_b, sgu_g, sgu_b, sgu_w, sgu_bias,
                 *, sub, alpha, seq):
    rows, d = x2.shape
    tiles_per_seq = seq // ROW_TILE
    return pl.pallas_call(
        functools.partial(_cd_kernel, sub=sub, alpha=alpha, tiles_per_seq=tiles_per_seq),
        grid=(rows // ROW_TILE,),
        in_specs=[
            pl.BlockSpec((ROW_TILE, d), lambda i: (i, 0)),
            pl.BlockSpec((None, 9, d), lambda i: (i // tiles_per_seq, 0, 0)),
            _const_spec((1, d)), _const_spec((1, d)),
            _const_spec(w_in.shape), _const_spec(w_out.shape),
            _const_spec(conv_w.shape), _const_spec(conv_b.shape),
            _const_spec(sgu_g.shape), _const_spec(sgu_b.shape),
            _const_spec(sgu_w.shape), _const_spec(sgu_bias.shape),
        ],
        out_specs=pl.BlockSpec((ROW_TILE, d), lambda i: (i, 0)),
        out_shape=jax.ShapeDtypeStruct((rows, d), F32),
        scratch_shapes=[pltpu.VMEM((ROW_TILE + 2 * SUBLANES, CONV_CH), F32)],
        compiler_params=_params(("arbitrary",)),
        name="mixer_cd_sublayer",
    )(x2, mod, ln_g, ln_b, w_in, w_out, conv_w, conv_b, sgu_g, sgu_b, sgu_w, sgu_bias)


def kernel(x, c, rel_bias, ada_w, ada_b, ln_g, ln_b, ffn_w_gate, ffn_w_up, ffn_w_down, ab_w_in, ab_w_out,
           ab_b_igate, ab_b_fgate, cd_w_in, cd_w_out, cd_conv_w, cd_conv_b, cd_sgu_ln_g, cd_sgu_ln_b,
           cd_sgu_w, cd_sgu_b):
    batch, seq, d = x.shape
    depth = ada_w.shape[0]
    alpha = (2 * depth) ** 0.25
    assert seq % ROW_TILE == 0 and ROW_TILE % SGU_CHUNK == 0
    assert all(seq % (dil * ATTN_BLOCK) == 0 and win // dil == ATTN_BLOCK for win, dil in DILATED_BRANCHES)

    mods = _modulation(c, ada_w, ada_b)
    x2 = x.reshape(batch * seq, d)
    for layer in range(depth):
        mod = mods[layer]
        lg = ln_g[layer][:, None, :]
        lb = ln_b[layer][:, None, :]
        ffn = functools.partial(_ffn_sublayer, wg=ffn_w_gate, wu=ffn_w_up, wd=ffn_w_down, layer=layer,
                                alpha=alpha, seq=seq)
        x2 = ffn(x2, mod, lg[0], lb[0], which=0, sub=0)
        if layer % 2 == 0:
            e = layer // 2
            gate_bias = jnp.concatenate(
                [ab_b_igate[e], ab_b_fgate[e], jnp.zeros((GATE_W - 2 * MLSTM_HEADS,), F32)])[None, :]
            mq, mk, mkt, mv, mo, grow, cfb, aq, ak, av = _ab_projection(
                x2, mod, _pack_ab_w_in(ab_w_in[e]), gate_bias, sub=1, seq=seq)
            ha = _mlstm(mq, mk, mkt, mv, mo, grow, cfb, batch=batch, seq=seq)
            hb = _dilated_attention(aq, ak, av, rel_bias, batch=batch, seq=seq)
            x2 = _ab_output(x2, mod, lg[1], lb[1], ha, hb, ab_w_out[e],
                            sub=1, alpha=alpha, seq=seq)
        else:
            o = layer // 2
            sgu_bias = jnp.broadcast_to(cd_sgu_b[o][:, :, None], (SGU_GROUPS, SGU_CHUNK, SGU_GROUP_CH))
            x2 = _cd_sublayer(x2, mod, lg[1], lb[1], cd_w_in[o], cd_w_out[o],
                              cd_conv_w[o], cd_conv_b[o][None, :], cd_sgu_ln_g[o].reshape(1, SGU_CH),
                              cd_sgu_ln_b[o].reshape(1, SGU_CH), cd_sgu_w[o], sgu_bias,
                              sub=1, alpha=alpha, seq=seq)
        x2 = ffn(x2, mod, lg[2], lb[2], which=1, sub=2)
    return x2.reshape(batch, seq, d)
```

```python
import functools
import math

import jax
import jax.numpy as jnp
import numpy as np
from jax import lax
from jax.experimental import pallas as pl
from jax.experimental.pallas import tpu as pltpu

F32 = jnp.float32
BF16 = jnp.bfloat16

MLSTM_HEADS = 4
MLSTM_QK_DIM = 64
MLSTM_V_DIM = 128
MLSTM_CHUNK = 128
ATTN_HEADS = 8
ATTN_HEAD_DIM = 64
DILATED_BRANCHES = ((128, 1), (512, 4), (2048, 16))
ATTN_BLOCK = 128
REL_BUCKETS = 32
REL_MAX_DIST = 2048
CONV_CH = 512
CONV_K = 3
SGU_CH = 512
SGU_GROUPS = 4
SGU_GROUP_CH = SGU_CH // SGU_GROUPS
SGU_CHUNK = 128
FFN_RES_W = 0.5
LN_EPS = 1e-5
MASK_VALUE = -1e30
LOG2_E = math.log2(math.e)

MLSTM_QK_W = MLSTM_HEADS * MLSTM_QK_DIM
MLSTM_V_W = MLSTM_HEADS * MLSTM_V_DIM
ATTN_W = ATTN_HEADS * ATTN_HEAD_DIM
GATE_W = 128

LANES = 128
SUBLANES = 8
VMEM_LIMIT_BYTES = 56 * 1024 * 1024

ROW_TILE = 512
FF_CHUNK = 256
MLSTM_CHUNKS_PER_STEP = 8


def _const_spec(shape):
    nd = len(shape)
    return pl.BlockSpec(shape, lambda *_: (0,) * nd, pipeline_mode=pl.Buffered(1))


def _params(sem):
    return pltpu.CompilerParams(dimension_semantics=sem, vmem_limit_bytes=VMEM_LIMIT_BYTES)


def _modulate(x, mod_ref, sub):
    shift = mod_ref[3 * sub:3 * sub + 1, :]
    scale = mod_ref[3 * sub + 1:3 * sub + 2, :]
    return x * (1.0 + scale) + shift


def _post_norm(x, y, mod_ref, sub, g_ref, b_ref, alpha, res_w):
    gate = mod_ref[3 * sub + 2:3 * sub + 3, :]
    r = alpha * x + res_w * (1.0 + gate) * y
    mu = jnp.mean(r, axis=-1, keepdims=True)
    rc = r - mu
    var = jnp.mean(rc * rc, axis=-1, keepdims=True)
    return rc * lax.rsqrt(var + LN_EPS) * g_ref[...] + b_ref[...]


def _mod_kernel(c_ref, w_ref, b_ref, o_ref):
    c = c_ref[...]
    sc = c * jax.nn.sigmoid(c)
    o_ref[...] = jnp.dot(sc, w_ref[...], preferred_element_type=F32) + b_ref[...]


def _modulation(c, ada_w, ada_b):
    depth, d, n = ada_w.shape
    b = c.shape[0]
    bp = -(-b // SUBLANES) * SUBLANES
    cp = jnp.pad(c, ((0, bp - b), (0, 0)))
    bn = 1024
    out = pl.pallas_call(
        _mod_kernel,
        grid=(depth, n // bn),
        in_specs=[
            pl.BlockSpec((bp, d), lambda l, j: (0, 0)),
            pl.BlockSpec((None, d, bn), lambda l, j: (l, 0, j)),
            pl.BlockSpec((None, 1, bn), lambda l, j: (l, 0, j)),
        ],
        out_specs=pl.BlockSpec((None, bp, bn), lambda l, j: (l, 0, j)),
        out_shape=jax.ShapeDtypeStruct((depth, bp, n), F32),
        compiler_params=_params(("arbitrary", "arbitrary")),
        name="adaln_mod",
    )(cp, ada_w, ada_b.reshape(depth, 1, n))
    return out[:, :b].reshape(depth, b, 9, d)


def _ffn_kernel(x_ref, mod_ref, g_ref, b_ref, wg_ref, wu_ref, wd_ref, o_ref, *, sub, alpha):
    x = x_ref[...]
    h = _modulate(x, mod_ref, sub).astype(BF16)
    d_ff = wg_ref.shape[1]
    acc = jnp.zeros(x.shape, F32)
    for c0 in range(0, d_ff, FF_CHUNK):
        g = jnp.dot(h, wg_ref[:, c0:c0 + FF_CHUNK].astype(BF16), preferred_element_type=F32)
        u = jnp.dot(h, wu_ref[:, c0:c0 + FF_CHUNK].astype(BF16), preferred_element_type=F32)
        a = (g * jax.nn.sigmoid(g) * u).astype(BF16)
        acc = acc + jnp.dot(a, wd_ref[c0:c0 + FF_CHUNK, :].astype(BF16), preferred_element_type=F32)
    o_ref[...] = _post_norm(x, acc, mod_ref, sub, g_ref, b_ref, alpha, FFN_RES_W)


def _ffn_sublayer(x2, mod, ln_g, ln_b, wg, wu, wd, *, layer, which, sub, alpha, seq):
    rows, d = x2.shape
    d_ff = wg.shape[-1]
    tiles_per_seq = seq // ROW_TILE

    def weight_spec(shape):
        return pl.BlockSpec((None, None) + shape, lambda i: (layer, which, 0, 0), pipeline_mode=pl.Buffered(1))

    return pl.pallas_call(
        functools.partial(_ffn_kernel, sub=sub, alpha=alpha),
        grid=(rows // ROW_TILE,),
        in_specs=[
            pl.BlockSpec((ROW_TILE, d), lambda i: (i, 0)),
            pl.BlockSpec((None, 9, d), lambda i: (i // tiles_per_seq, 0, 0)),
            _const_spec((1, d)),
            _const_spec((1, d)),
            weight_spec((d, d_ff)),
            weight_spec((d, d_ff)),
            weight_spec((d_ff, d)),
        ],
        out_specs=pl.BlockSpec((ROW_TILE, d), lambda i: (i, 0)),
        out_shape=jax.ShapeDtypeStruct((rows, d), F32),
        compiler_params=_params(("arbitrary",)),
        name="swiglu_sublayer",
    )(x2, mod, ln_g, ln_b, wg, wu, wd)


_AB_SEGS = (
    ("mq", MLSTM_QK_W, BF16, 1.0),
    ("mk", MLSTM_QK_W, BF16, MLSTM_QK_DIM ** -0.5),
    ("mv", MLSTM_V_W, BF16, 1.0),
    ("mo", MLSTM_V_W, F32, 1.0),
    ("gates", GATE_W, F32, 1.0),
    ("aq", ATTN_W, F32, ATTN_HEAD_DIM ** -0.5 * LOG2_E),
    ("ak", ATTN_W, F32, 1.0),
    ("av", ATTN_W, F32, 1.0),
)
_AB_PACKED_W = sum(s[1] for s in _AB_SEGS)
_AB_SLAB_SEGS = ("aq", "ak", "av")
ATTN_SLABS = ATTN_W // LANES


def _pack_ab_w_in(w_in):
    d = w_in.shape[0]
    sizes = (MLSTM_QK_W, MLSTM_QK_W, MLSTM_V_W, MLSTM_V_W, MLSTM_HEADS, MLSTM_HEADS, ATTN_W, ATTN_W, ATTN_W)
    mq, mk, mv, mo, mi, mf, aq, ak, av = jnp.split(w_in, np.cumsum(sizes)[:-1].tolist(), axis=1)
    gates = jnp.concatenate([mi, mf, jnp.zeros((d, GATE_W - 2 * MLSTM_HEADS), w_in.dtype)], axis=1)
    return jnp.concatenate([mq, mk, mv, mo, gates, aq, ak, av], axis=1).astype(BF16)


def _ab_proj_kernel(x_ref, mod_ref, w_ref, gbias_ref, mq_ref, mk_ref, mkt_ref, mv_ref, mo_ref, grow_ref, cfb_ref,
                    aq_ref, ak_ref, av_ref, *, sub):
    L = MLSTM_CHUNK
    H = MLSTM_HEADS
    tm = x_ref.shape[0]
    h = _modulate(x_ref[...], mod_ref, sub).astype(BF16)

    def proj(name):
        off = 0
        for seg, width, _, scale in _AB_SEGS:
            if seg == name:
                z = jnp.dot(h, w_ref[:, off:off + width], preferred_element_type=F32)
                return z if scale == 1.0 else z * scale
            off += width
        raise KeyError(name)

    gb = proj("gates") + gbias_ref[...]
    log_f = -(jnp.maximum(-gb, 0.0) + jnp.log1p(jnp.exp(-jnp.abs(gb))))
    row = lax.broadcasted_iota(jnp.int32, (L, L), 0)
    col = lax.broadcasted_iota(jnp.int32, (L, L), 1)
    tril = (row >= col).astype(F32)
    is_f = (col >= H) & (col < 2 * H)

    zk = proj("mk")
    mk_ref[...] = zk.astype(BF16)
    mq_ref[...] = proj("mq").astype(BF16)

    gate_cols = []
    for c in range(tm // L):
        ch = slice(c * L, (c + 1) * L)
        cum_f = jnp.dot(tril, jnp.where(is_f, log_f[ch], 0.0), precision=lax.Precision.HIGHEST,
                        preferred_element_type=F32)
        gate_cols.append(jnp.where(col < H, gb[ch], cum_f))

    mv_ref[...] = proj("mv").astype(BF16)
    kw = MLSTM_QK_W
    for c in range(tm // L):
        mkt_ref[c * kw:(c + 1) * kw, :] = zk[c * L:(c + 1) * L, :].T.astype(BF16)
    mo_ref[...] = proj("mo")

    for c, a in enumerate(gate_cols):
        grow_ref[c * SUBLANES:(c + 1) * SUBLANES, :] = a.T[0:SUBLANES, :]
        for hd in range(H):
            cfb_ref[c * L:(c + 1) * L, hd * LANES:(hd + 1) * LANES] = jnp.broadcast_to(
                a[:, H + hd:H + hd + 1], (L, LANES))

    for name, o_ref in (("aq", aq_ref), ("ak", ak_ref), ("av", av_ref)):
        z = proj(name)
        for s in range(ATTN_SLABS):
            o_ref[s] = z[:, s * LANES:(s + 1) * LANES]

def _ab_projection(x2, mod, w_packed, gate_bias, *, sub, seq):
    rows, d = x2.shape
    tiles_per_seq = seq // ROW_TILE
    chunks = ROW_TILE // MLSTM_CHUNK
    row_block = lambda r, w: pl.BlockSpec((r, w), lambda i: (i, 0))
    slab_block = pl.BlockSpec((ATTN_SLABS, ROW_TILE, LANES), lambda i: (0, i, 0))
    slab_shape = jax.ShapeDtypeStruct((ATTN_SLABS, rows, LANES), F32)
    outs = [
        (row_block(ROW_TILE, MLSTM_QK_W), jax.ShapeDtypeStruct((rows, MLSTM_QK_W), BF16)),
        (row_block(ROW_TILE, MLSTM_QK_W), jax.ShapeDtypeStruct((rows, MLSTM_QK_W), BF16)),
        (row_block(chunks * MLSTM_QK_W, MLSTM_CHUNK),
         jax.ShapeDtypeStruct((rows // MLSTM_CHUNK * MLSTM_QK_W, MLSTM_CHUNK), BF16)),
        (row_block(ROW_TILE, MLSTM_V_W), jax.ShapeDtypeStruct((rows, MLSTM_V_W), BF16)),
        (row_block(ROW_TILE, MLSTM_V_W), jax.ShapeDtypeStruct((rows, MLSTM_V_W), F32)),
        (row_block(chunks * SUBLANES, MLSTM_CHUNK),
         jax.ShapeDtypeStruct((rows // MLSTM_CHUNK * SUBLANES, MLSTM_CHUNK), F32)),
        (row_block(ROW_TILE, MLSTM_HEADS * LANES), jax.ShapeDtypeStruct((rows, MLSTM_HEADS * LANES), F32)),
        (slab_block, slab_shape), (slab_block, slab_shape), (slab_block, slab_shape),
    ]
    return pl.pallas_call(
        functools.partial(_ab_proj_kernel, sub=sub),
        grid=(rows // ROW_TILE,),
        in_specs=[
            pl.BlockSpec((ROW_TILE, d), lambda i: (i, 0)),
            pl.BlockSpec((None, 9, d), lambda i: (i // tiles_per_seq, 0, 0)),
            _const_spec((d, _AB_PACKED_W)),
            _const_spec((1, GATE_W)),
        ],
        out_specs=[o[0] for o in outs],
        out_shape=[o[1] for o in outs],
        compiler_params=_params(("arbitrary",)),
        name="mixer_ab_in_proj",
    )(x2, mod, w_packed, gate_bias)


def _mlstm_kernel(q_ref, k_ref, kt_ref, v_ref, og_ref, grow_ref, cfb_ref, out_ref, c_ref, m_ref):
    L = MLSTM_CHUNK
    H = MLSTM_HEADS
    DK = MLSTM_QK_DIM
    DV = MLSTM_V_DIM

    @pl.when(pl.program_id(1) == 0)
    def _():
        c_ref[...] = jnp.zeros_like(c_ref)
        m_ref[...] = jnp.zeros_like(m_ref)

    causal = lax.broadcasted_iota(jnp.int32, (L, L), 0) >= lax.broadcasted_iota(jnp.int32, (L, L), 1)
    ones = jnp.ones((L, LANES), BF16)

    state = [(c_ref[h], m_ref[h][0:1, 0:1]) for h in range(H)]
    for c, h in ((c, h) for c in range(q_ref.shape[0] // L) for h in range(H)):
        t = slice(c * L, (c + 1) * L)
        q = q_ref[t, h * DK:(h + 1) * DK]
        k = k_ref[t, h * DK:(h + 1) * DK]
        v1 = jnp.concatenate([v_ref[t, h * DV:(h + 1) * DV], ones], axis=1)
        li_row = grow_ref[c * SUBLANES + h:c * SUBLANES + h + 1, :]
        cf_row = grow_ref[c * SUBLANES + H + h:c * SUBLANES + H + h + 1, :]
        cf = cfb_ref[t, h * LANES:(h + 1) * LANES]
        cn_prev, m_prev = state[h]

        d_log = jnp.where(causal, cf - cf_row + li_row, -jnp.inf)
        inter_log = cf + m_prev
        m_t = jnp.maximum(inter_log, jnp.broadcast_to(jnp.max(d_log, axis=1, keepdims=True), (L, L)))
        s = lax.dot_general(q, k, (((1,), (1,)), ((), ())), preferred_element_type=F32)
        w_intra = (jnp.exp(d_log - m_t) * s).astype(BF16)
        w_inter = jnp.exp(inter_log - m_t)
        intra = jnp.dot(w_intra, v1, preferred_element_type=F32)
        inter = jnp.dot(q, cn_prev.astype(BF16), preferred_element_type=F32)
        num = intra[:, :DV] + inter[:, :DV] * w_inter
        den = intra[:, DV:] + inter[:, DV:] * w_inter
        den = jnp.maximum(jnp.abs(den), jnp.exp(-m_t))
        o_gate = jax.nn.sigmoid(og_ref[t, h * DV:(h + 1) * DV])
        out_ref[t, h * DV:(h + 1) * DV] = o_gate * (num / den)

        chunk_f = cf_row[:, L - 1:L]
        g_row = chunk_f - cf_row + li_row
        m_new = jnp.maximum(chunk_f + m_prev, jnp.max(g_row, axis=1, keepdims=True))
        decay = jnp.exp(chunk_f + m_prev - m_new)
        k_t = kt_ref[c * H * DK + h * DK:c * H * DK + (h + 1) * DK, :]
        kw_t = (k_t.astype(F32) * jnp.exp(g_row - m_new)).astype(BF16)
        state[h] = (decay * cn_prev + jnp.dot(kw_t, v1, preferred_element_type=F32), m_new)

    for h in range(H):
        c_ref[h] = state[h][0]
        m_ref[h] = jnp.broadcast_to(state[h][1], m_ref.shape[1:])


def _mlstm(mq, mk, mkt, mv, mo, grow, cfb, *, batch, seq):
    L = MLSTM_CHUNK
    n = MLSTM_CHUNKS_PER_STEP
    steps = seq // (n * L)
    assert MLSTM_V_DIM == LANES and MLSTM_CHUNK == LANES and 2 * MLSTM_HEADS <= SUBLANES and seq % (n * L) == 0

    def spec(r, w):
        return pl.BlockSpec((n * r, w), lambda b, c: (b * steps + c, 0))

    return pl.pallas_call(
        _mlstm_kernel,
        grid=(batch, steps),
        in_specs=[spec(L, MLSTM_QK_W), spec(L, MLSTM_QK_W), spec(MLSTM_QK_W, L), spec(L, MLSTM_V_W),
                  spec(L, MLSTM_V_W), spec(SUBLANES, L), spec(L, MLSTM_HEADS * LANES)],
        out_specs=spec(L, MLSTM_V_W),
        out_shape=jax.ShapeDtypeStruct((batch * seq, MLSTM_V_W), F32),
        scratch_shapes=[
            pltpu.VMEM((MLSTM_HEADS, MLSTM_QK_DIM, MLSTM_V_DIM + LANES), F32),
            pltpu.VMEM((MLSTM_HEADS, SUBLANES, LANES), F32),
        ],
        compiler_params=_params(("arbitrary", "arbitrary")),
        name="mlstm_chunkwise",
    )(mq, mk, mkt, mv, mo, grow, cfb)


def _rel_bucket(dist):
    max_exact = REL_BUCKETS // 2
    d = jnp.maximum(dist, 0)
    large = max_exact + (jnp.log(jnp.maximum(d, 1).astype(F32) / max_exact)
                         / math.log(REL_MAX_DIST / max_exact) * (REL_BUCKETS - max_exact)).astype(jnp.int32)
    large = jnp.minimum(large, REL_BUCKETS - 1)
    return jnp.where(d < max_exact, d, large)


def _branch_bias(rel_bias, window, dilation):
    blk = ATTN_BLOCK
    span = window // dilation
    qi = jnp.arange(blk)[:, None]
    kj = jnp.arange(2 * blk)[None, :]
    dist = qi + blk - kj
    valid = (dist >= 0) & (dist <= span)
    one_hot = (_rel_bucket(dist * dilation)[..., None] == jnp.arange(REL_BUCKETS)).astype(F32)
    bias = jnp.einsum('qkb,bh->hqk', one_hot, rel_bias.astype(F32), precision=lax.Precision.HIGHEST)
    return jnp.where(valid[None], bias * LOG2_E, MASK_VALUE)


ATTN_GROUP = ATTN_BLOCK * max(d for _, d in DILATED_BRANCHES)
HEADS_PER_SLAB = LANES // ATTN_HEAD_DIM
FAR_UNROLL = 8
MID_UNROLL = 2
NEAR_UNROLL = 5


def _attn_kernel(q_ref, kp_ref, k_ref, vp_ref, v_ref, bias_ref, out_ref, m_s, l_s, acc_s):
    blk = ATTN_BLOCK
    grp = ATTN_GROUP
    first_group = pl.program_id(2) == 0

    head_a = lax.broadcasted_iota(jnp.int32, (blk, LANES), 1) < ATTN_HEAD_DIM
    no_prev = jnp.where(first_group, MASK_VALUE, 0.0)
    nt = (((1,), (1,)), ((), ()))

    def unit(branch, d, start, *, first, last, prev_in_prev_group):
        def rows(base):
            return pl.ds(base, blk, stride=d) if d > 1 else pl.ds(base, blk)

        q = q_ref[rows(start), :]
        q2 = jnp.concatenate([jnp.where(head_a, q, 0.0), jnp.where(head_a, 0.0, q)], axis=0).astype(BF16)
        if prev_in_prev_group:
            k_prev, v_prev = kp_ref[rows(grp + start - blk * d), :], vp_ref[rows(grp + start - blk * d), :]
        else:
            k_prev, v_prev = k_ref[rows(start - blk * d), :], v_ref[rows(start - blk * d), :]
        kpc = jnp.concatenate([k_prev, k_ref[rows(start), :]], axis=0).astype(BF16)
        vpc = jnp.concatenate([v_prev, v_ref[rows(start), :]], axis=0).astype(BF16)
        s = lax.dot_general(q2, kpc, nt, preferred_element_type=F32) + bias_ref[branch]
        if prev_in_prev_group:
            s = jnp.concatenate([s[:, :blk] + no_prev, s[:, blk:]], axis=1)
        ps, maxes, sums = [], [], []
        for hh in range(HEADS_PER_SLAB):
            sh = s[hh * blk:(hh + 1) * blk, :]
            mx = jnp.max(sh, axis=1, keepdims=True)
            p = jnp.exp2(sh - mx)
            ps.append(p)
            maxes.append(jnp.broadcast_to(mx, (blk, LANES)))
            sums.append(jnp.broadcast_to(jnp.sum(p, axis=1, keepdims=True), (blk, LANES)))
        m_new = jnp.where(head_a, maxes[0], maxes[1])
        l_new = jnp.where(head_a, sums[0], sums[1])
        pv = jnp.dot(jnp.concatenate(ps, axis=0).astype(BF16), vpc, preferred_element_type=F32)
        acc = jnp.where(head_a, pv[:blk], pv[blk:])
        if not first:
            m_old, m_blk = m_s[rows(start), :], m_new
            m_new = jnp.maximum(m_old, m_blk)
            alpha = jnp.exp2(m_old - m_new)
            beta = jnp.exp2(m_blk - m_new)
            l_new = alpha * l_s[rows(start), :] + beta * l_new
            acc = alpha * acc_s[rows(start), :] + beta * acc
        if last:
            out_ref[rows(start), :] = acc / l_new
        else:
            m_s[rows(start), :] = m_new
            l_s[rows(start), :] = l_new
            acc_s[rows(start), :] = acc

    (_, d_near), (_, d_mid), (_, d_far) = DILATED_BRANCHES

    def far_body(i, carry):
        for u in range(FAR_UNROLL):
            unit(2, d_far, i * FAR_UNROLL + u, first=True, last=False, prev_in_prev_group=True)
        return carry

    lax.fori_loop(0, d_far // FAR_UNROLL, far_body, 0)

    blocks_mid = grp // (d_mid * blk)

    def mid_body(i, carry):
        for u in range(MID_UNROLL):
            for j in range(blocks_mid):
                unit(1, d_mid, i * MID_UNROLL + u + j * d_mid * blk, first=False, last=False,
                     prev_in_prev_group=(j == 0))
        return carry

    lax.fori_loop(0, d_mid // MID_UNROLL, mid_body, 0)

    unit(0, d_near, 0, first=False, last=True, prev_in_prev_group=True)

    def near_body(i, carry):
        for u in range(NEAR_UNROLL):
            start = pl.multiple_of((1 + i * NEAR_UNROLL + u) * blk, blk)
            unit(0, d_near, start, first=False, last=True, prev_in_prev_group=False)
        return carry

    lax.fori_loop(0, (grp // blk - 1) // NEAR_UNROLL, near_body, 0)


def _dilated_attention(aq, ak, av, rel_bias, *, batch, seq):
    grp = ATTN_GROUP
    blk = ATTN_BLOCK
    assert [d for _, d in DILATED_BRANCHES] == [1, 4, 16] and (grp // blk - 1) % NEAR_UNROLL == 0
    groups = seq // grp
    rows = batch * seq
    bias = jnp.stack([_branch_bias(rel_bias, w, d).reshape(ATTN_SLABS, HEADS_PER_SLAB * blk, 2 * blk)
                      for w, d in DILATED_BRANCHES])
    slab = pl.BlockSpec((None, grp, LANES), lambda s, b, g: (s, b * groups + g, 0))
    prev = pl.BlockSpec((None, grp, LANES), lambda s, b, g: (s, b * groups + jnp.maximum(g - 1, 0), 0))
    return pl.pallas_call(
        _attn_kernel,
        grid=(ATTN_SLABS, batch, groups),
        in_specs=[slab, prev, slab, prev, slab,
                  pl.BlockSpec((len(DILATED_BRANCHES), None, HEADS_PER_SLAB * blk, 2 * blk),
                               lambda s, b, g: (0, s, 0, 0))],
        out_specs=slab,
        out_shape=jax.ShapeDtypeStruct((ATTN_SLABS, rows, LANES), F32),
        scratch_shapes=[pltpu.VMEM((grp, LANES), F32)] * 3,
        compiler_params=_params(("arbitrary", "arbitrary", "arbitrary")),
        name="dilated_attention",
    )(aq, ak, ak, av, av, bias)


def _ab_out_kernel(x_ref, mod_ref, g_ref, b_ref, ha_ref, hb_ref, w_ref, out_ref, *, sub, alpha):
    hb = jnp.concatenate([hb_ref[s] for s in range(ATTN_SLABS)], axis=1)
    wa = MLSTM_V_W
    y = (jnp.dot(ha_ref[...].astype(BF16), w_ref[:wa, :].astype(BF16), preferred_element_type=F32)
         + jnp.dot(hb.astype(BF16), w_ref[wa:, :].astype(BF16), preferred_element_type=F32))
    out_ref[...] = _post_norm(x_ref[...], y, mod_ref, sub, g_ref, b_ref, alpha, 1.0)


def _ab_output(x2, mod, ln_g, ln_b, ha, hb, w_out, *, sub, alpha, seq):
    rows, d = x2.shape
    tiles_per_seq = seq // ROW_TILE
    row_spec = lambda w: pl.BlockSpec((ROW_TILE, w), lambda i: (i, 0))
    return pl.pallas_call(
        functools.partial(_ab_out_kernel, sub=sub, alpha=alpha),
        grid=(rows // ROW_TILE,),
        in_specs=[row_spec(d),
                  pl.BlockSpec((None, 9, d), lambda i: (i // tiles_per_seq, 0, 0)),
                  _const_spec((1, d)), _const_spec((1, d)),
                  row_spec(MLSTM_V_W),
                  pl.BlockSpec((ATTN_SLABS, ROW_TILE, LANES), lambda i: (0, i, 0)),
                  _const_spec(w_out.shape)],
        out_specs=row_spec(d),
        out_shape=jax.ShapeDtypeStruct((rows, d), F32),
        compiler_params=_params(("arbitrary",)),
        name="mixer_ab_out_proj",
    )(x2, mod, ln_g, ln_b, ha, hb, w_out)


def _gelu_tanh(x):
    return 0.5 * x * (1.0 + jnp.tanh(math.sqrt(2.0 / math.pi) * (x + 0.044715 * (x * x * x))))


def _cd_kernel(x_ref, mod_ref, g_ref, b_ref, w_in_ref, w_out_ref, conv_w_ref, conv_b_ref,
               sgu_g_ref, sgu_b_ref, sgu_w_ref, sgu_bias_ref, out_ref, conv_ref, *, sub, alpha, tiles_per_seq):
    tm = x_ref.shape[0]
    halo = SUBLANES

    @pl.when(pl.program_id(0) % tiles_per_seq == 0)
    def _():
        conv_ref[0:halo, :] = jnp.zeros((halo, CONV_CH), F32)

    x = x_ref[...]
    h = _modulate(x, mod_ref, sub).astype(BF16)

    def proj(idx, width):
        return jnp.dot(h, w_in_ref[:, idx:idx + width].astype(BF16), preferred_element_type=F32)

    gate_c = proj(CONV_CH, CONV_CH)
    xc = proj(2 * CONV_CH, CONV_CH)
    v = proj(3 * CONV_CH + SGU_CH, SGU_CH)

    conv_ref[halo:halo + tm, :] = gate_c * xc
    conv = conv_b_ref[...]
    for j in range(CONV_K):
        start = halo - (CONV_K - 1) + j
        conv = conv + conv_w_ref[j:j + 1, :] * conv_ref[start:start + tm, :]
    conv_ref[0:halo, :] = conv_ref[tm:tm + halo, :]

    gate_b = proj(0, CONV_CH)
    v = _gelu_tanh(v)
    row = lax.broadcasted_iota(jnp.int32, (SGU_CHUNK, SGU_CHUNK), 0)
    col = lax.broadcasted_iota(jnp.int32, (SGU_CHUNK, SGU_CHUNK), 1)
    vns = []
    for g in range(SGU_GROUPS):
        gs = slice(g * SGU_GROUP_CH, (g + 1) * SGU_GROUP_CH)
        vg = v[:, gs]
        mu = jnp.mean(vg, axis=-1, keepdims=True)
        vc = vg - mu
        var = jnp.mean(vc * vc, axis=-1, keepdims=True)
        vns.append((vc * lax.rsqrt(var + LN_EPS) * sgu_g_ref[:, gs] + sgu_b_ref[:, gs]).astype(BF16))

    u = proj(3 * CONV_CH, SGU_CH)
    y = jnp.dot((gate_b * conv).astype(BF16), w_out_ref[:CONV_CH, :].astype(BF16), preferred_element_type=F32)
    u = _gelu_tanh(u)
    y_d_groups = []
    for g in range(SGU_GROUPS):
        gs = slice(g * SGU_GROUP_CH, (g + 1) * SGU_GROUP_CH)
        w_s = jnp.where(row >= col, sgu_w_ref[g], 0.0).astype(BF16)
        mixed = [jnp.dot(w_s, vns[g][t0:t0 + SGU_CHUNK, :], preferred_element_type=F32) + sgu_bias_ref[g]
                 for t0 in range(0, tm, SGU_CHUNK)]
        y_d_groups.append(u[:, gs] * jnp.concatenate(mixed, axis=0))
    y_d = jnp.concatenate(y_d_groups, axis=1)

    y = y + jnp.dot(y_d.astype(BF16), w_out_ref[CONV_CH:, :].astype(BF16), preferred_element_type=F32)
    out_ref[...] = _post_norm(x, y, mod_ref, sub, g_ref, b_ref, alpha, 1.0)


def _cd_sublayer(x2, mod, ln_g, ln_b, w_in, w_out, conv_w, conv_b, sgu_g, sgu_b, sgu_w, sgu_bias,
                 *, sub, alpha, seq):
    rows, d = x2.shape
    tiles_per_seq = seq // ROW_TILE
    return pl.pallas_call(
        functools.partial(_cd_kernel, sub=sub, alpha=alpha, tiles_per_seq=tiles_per_seq),
        grid=(rows // ROW_TILE,),
        in_specs=[
            pl.BlockSpec((ROW_TILE, d), lambda i: (i, 0)),
            pl.BlockSpec((None, 9, d), lambda i: (i // tiles_per_seq, 0, 0)),
            _const_spec((1, d)), _const_spec((1, d)),
            _const_spec(w_in.shape), _const_spec(w_out.shape),
            _const_spec(conv_w.shape), _const_spec(conv_b.shape),
            _const_spec(sgu_g.shape), _const_spec(sgu_b.shape),
            _const_spec(sgu_w.shape), _const_spec(sgu_bias.shape),
        ],
        out_specs=pl.BlockSpec((ROW_TILE, d), lambda i: (i, 0)),
        out_shape=jax.ShapeDtypeStruct((rows, d), F32),
        scratch_shapes=[pltpu.VMEM((ROW_TILE + 2 * SUBLANES, CONV_CH), F32)],
        compiler_params=_params(("arbitrary",)),
        name="mixer_cd_sublayer",
    )(x2, mod, ln_g, ln_b, w_in, w_out, conv_w, conv_b, sgu_g, sgu_b, sgu_w, sgu_bias)


def kernel(x, c, rel_bias, ada_w, ada_b, ln_g, ln_b, ffn_w_gate, ffn_w_up, ffn_w_down, ab_w_in, ab_w_out,
           ab_b_igate, ab_b_fgate, cd_w_in, cd_w_out, cd_conv_w, cd_conv_b, cd_sgu_ln_g, cd_sgu_ln_b,
           cd_sgu_w, cd_sgu_b):
    batch, seq, d = x.shape
    depth = ada_w.shape[0]
    alpha = (2 * depth) ** 0.25
    assert seq % ROW_TILE == 0 and ROW_TILE % SGU_CHUNK == 0
    assert all(seq % (dil * ATTN_BLOCK) == 0 and win // dil == ATTN_BLOCK for win, dil in DILATED_BRANCHES)

    mods = _modulation(c, ada_w, ada_b)
    x2 = x.reshape(batch * seq, d)
    for layer in range(depth):
        mod = mods[layer]
        lg = ln_g[layer][:, None, :]
        lb = ln_b[layer][:, None, :]
        ffn = functools.partial(_ffn_sublayer, wg=ffn_w_gate, wu=ffn_w_up, wd=ffn_w_down, layer=layer,
                                alpha=alpha, seq=seq)
        x2 = ffn(x2, mod, lg[0], lb[0], which=0, sub=0)
        if layer % 2 == 0:
            e = layer // 2
            gate_bias = jnp.concatenate(
                [ab_b_igate[e], ab_b_fgate[e], jnp.zeros((GATE_W - 2 * MLSTM_HEADS,), F32)])[None, :]
            mq, mk, mkt, mv, mo, grow, cfb, aq, ak, av = _ab_projection(
                x2, mod, _pack_ab_w_in(ab_w_in[e]), gate_bias, sub=1, seq=seq)
            ha = _mlstm(mq, mk, mkt, mv, mo, grow, cfb, batch=batch, seq=seq)
            hb = _dilated_attention(aq, ak, av, rel_bias, batch=batch, seq=seq)
            x2 = _ab_output(x2, mod, lg[1], lb[1], ha, hb, ab_w_out[e],
                            sub=1, alpha=alpha, seq=seq)
        else:
            o = layer // 2
            sgu_bias = jnp.broadcast_to(cd_sgu_b[o][:, :, None], (SGU_GROUPS, SGU_CHUNK, SGU_GROUP_CH))
            x2 = _cd_sublayer(x2, mod, lg[1], lb[1], cd_w_in[o], cd_w_out[o],
                              cd_conv_w[o], cd_conv_b[o][None, :], cd_sgu_ln_g[o].reshape(1, SGU_CH),
                              cd_sgu_ln_b[o].reshape(1, SGU_CH), cd_sgu_w[o], sgu_bias,
                              sub=1, alpha=alpha, seq=seq)
        x2 = ffn(x2, mod, lg[2], lb[2], which=1, sub=2)
    return x2.reshape(batch, seq, d)
```
